```python
import jax, jax.numpy as jnp
from jax import lax
import numpy as np

D_MODEL = 4096
BATCH = 2
SEQ = 8192
DEPTH = 2

CTX_LEN = 256
GRID_W = 64
EPS = 1e-6

SSD_INNER = D_MODEL // 2
SSD_HEAD_DIM = 64
SSD_HEADS = SSD_INNER // SSD_HEAD_DIM
SSD_GROUPS = 8
SSD_STATE = 128
SSD_GN = SSD_GROUPS * SSD_STATE
SSD_XBC = SSD_INNER + 2 * SSD_GN
SSD_CONV = 5
SSD_CHUNK = 128

POOL_WINDOWS = (2, 4, 8, 16)
POOL_GROUP = D_MODEL // 16
POOL_WIDTH = POOL_GROUP * len(POOL_WINDOWS)
POOL_OUT = D_MODEL // len(POOL_WINDOWS)

CONF_WIDTH = D_MODEL // 4
CONF_KERNEL = 31

N_BRANCHES = 3

N_EXPERTS = 16
N_EXPERT_GROUPS = 4
EXPERTS_PER_GROUP = N_EXPERTS // N_EXPERT_GROUPS
TOPK_GROUPS = 1
TOP_K = 2
EXPERT_FF = D_MODEL // 4

OFF_DT = SSD_XBC
OFF_Z = OFF_DT + 2 * SSD_HEADS
OFF_POOL = OFF_Z + SSD_INNER
OFF_CONF = OFF_POOL + POOL_WIDTH
OFF_GATE = OFF_CONF + 2 * CONF_WIDTH
D_IN_PROJ = OFF_GATE + N_BRANCHES * D_MODEL

kernel_name = "hybrid_ssd_pool_conformer_moe_dit"


def rms_norm(x, w):
    xf = x.astype(jnp.float32)
    y = xf * lax.rsqrt(jnp.mean(xf * xf, axis=-1, keepdims=True) + EPS)
    return (y * w.astype(jnp.float32)).astype(x.dtype)


def layer_norm(x, w, b):
    xf = x.astype(jnp.float32)
    mu = jnp.mean(xf, axis=-1, keepdims=True)
    xc = xf - mu
    y = xc * lax.rsqrt(jnp.mean(xc * xc, axis=-1, keepdims=True) + EPS)
    return (y * w.astype(jnp.float32) + b.astype(jnp.float32)).astype(x.dtype)


def dwconv(v, w, b):
    k, ch = w.shape
    out = lax.conv_general_dilated(v, w[:, None, :].astype(v.dtype), window_strides=(1,),
                                   padding=[(k // 2, k // 2)],
                                   dimension_numbers=('NWC', 'WIO', 'NWC'),
                                   feature_group_count=ch)
    return out + b.astype(v.dtype)


def column_conv(v, w, b):
    bsz, l, ch = v.shape
    rows = l // GRID_W
    cols = v.reshape(bsz, rows, GRID_W, ch).transpose(0, 2, 1, 3).reshape(bsz * GRID_W, rows, ch)
    out = dwconv(cols, w, b)
    return out.reshape(bsz, GRID_W, rows, ch).transpose(0, 2, 1, 3).reshape(bsz, l, ch)


def ssd_inputs(p, conv_w, conv_b):
    bsz, l, _ = p.shape
    xbc = jax.nn.silu(dwconv(p[..., :SSD_XBC], conv_w, conv_b))
    xs = xbc[..., :SSD_INNER].reshape(bsz, l, SSD_HEADS, SSD_HEAD_DIM)
    bg = xbc[..., SSD_INNER:SSD_INNER + SSD_GN].reshape(bsz, l, SSD_GROUPS, SSD_STATE)
    cg = xbc[..., SSD_INNER + SSD_GN:SSD_XBC].reshape(bsz, l, SSD_GROUPS, SSD_STATE)
    return xs, bg, cg, p[..., OFF_DT:OFF_Z]


def ssd_chunked(xs, dt, a_head, bg, cg, h0, with_y):
    bsz, seqlen, nh, hd = xs.shape
    ng, ns = bg.shape[2], bg.shape[3]
    r = nh // ng
    q = SSD_CHUNK
    nc = seqlen // q
    xdt = (xs * dt[..., None]).reshape(bsz, nc, q, ng, r, hd)
    a_cs = jnp.cumsum((dt * a_head).reshape(bsz, nc, q, ng, r), axis=2)
    a_tot = a_cs[:, :, -1]
    bc = bg.reshape(bsz, nc, q, ng, ns)
    cc = cg.reshape(bsz, nc, q, ng, ns)
    chunk_states = jnp.einsum('bcqgn,bcqgr,bcqgrp->bcgrpn', bc, jnp.exp(a_tot[:, :, None] - a_cs), xdt)

    def step(state, inp):
        s_c, a_c = inp
        return state * jnp.exp(a_c)[..., None, None] + s_c, state

    final, start = lax.scan(step, h0.reshape(bsz, ng, r, hd, ns),
                            (jnp.moveaxis(chunk_states, 1, 0), jnp.moveaxis(a_tot, 1, 0)))
    final = final.reshape(bsz, nh, hd, ns)
    if not with_y:
        return None, final
    start = jnp.moveaxis(start, 0, 1)
    seg = a_cs[:, :, :, None] - a_cs[:, :, None, :]
    lower = np.tril(np.ones((q, q), dtype=bool))[:, :, None, None]
    decay = jnp.exp(jnp.where(lower, seg, -jnp.inf))
    cb = jnp.einsum('bclgn,bcsgn->bclsg', cc, bc)
    y_diag = jnp.einsum('bclsg,bclsgr,bcsgrp->bclgrp', cb, decay, xdt)
    y_off = jnp.einsum('bclgn,bcgrpn,bclgr->bclgrp', cc, start, jnp.exp(a_cs))
    return (y_diag + y_off).reshape(bsz, seqlen, nh, hd), final


def ssd_direction(xs, bg, cg, dt_raw, h0, a_log, dt_bias, d_skip, reverse, with_y):
    f32 = jnp.float32
    xs, bg, cg = xs.astype(f32), bg.astype(f32), cg.astype(f32)
    dt = jax.nn.softplus(dt_raw.astype(f32) + dt_bias.astype(f32))
    if reverse:
        xs, bg, cg, dt = (jnp.flip(t, axis=1) for t in (xs, bg, cg, dt))
    a_head = -jnp.exp(a_log.astype(f32))
    y, final = ssd_chunked(xs, dt, a_head, bg, cg, h0, with_y)
    if with_y:
        y = y + d_skip.astype(f32)[:, None] * xs
        if reverse:
            y = jnp.flip(y, axis=1)
    return y, final


def window_mean_minus_self(v, window, row_len):
    bsz, l, ch = v.shape
    rows = v.reshape(bsz * (l // row_len), row_len, ch).astype(jnp.float32)
    j = np.arange(row_len)
    lo = np.clip(j - window // 2, 0, row_len - 1)
    hi = np.clip(j + window // 2 - 1, 0, row_len - 1)
    cs = jnp.pad(jnp.cumsum(rows, axis=1), ((0, 0), (1, 0), (0, 0)))
    mean = (cs[:, hi + 1] - cs[:, lo]) / (hi - lo + 1).astype(np.float32)[None, :, None]
    return (mean - rows).reshape(bsz, l, ch).astype(v.dtype)


def pool_branch(u, row_len, w_pool, pool_scale):
    parts = [window_mean_minus_self(u[..., k * POOL_GROUP:(k + 1) * POOL_GROUP], w, row_len)
             for k, w in enumerate(POOL_WINDOWS)]
    m = jnp.stack(parts, axis=-2)
    y = jnp.einsum('blgc,gco->blgo', m, w_pool)
    return y.reshape(u.shape[0], u.shape[1], D_MODEL) * pool_scale


def conformer_branch(u2, grid, conv_w, conv_b, ln_w, ln_b, w_conf_out):
    v = u2[..., :CONF_WIDTH] * jax.nn.sigmoid(u2[..., CONF_WIDTH:])
    v = column_conv(v, conv_w, conv_b) if grid else dwconv(v, conv_w, conv_b)
    v = jax.nn.silu(layer_norm(v, ln_w, ln_b))
    return v @ w_conf_out


def merge_branches(p, y_ssd, grid, ssd_norm_w, w_ssd_out, w_pool, pool_scale, conf_w, conf_b,
                   ln_w, ln_b, w_conf_out, w_out):
    dtype = p.dtype
    z = p[..., OFF_Z:OFF_POOL].astype(jnp.float32)
    y_a = rms_norm(y_ssd * jax.nn.silu(z), ssd_norm_w).astype(dtype) @ w_ssd_out
    row_len = GRID_W if grid else p.shape[1]
    y_b = pool_branch(p[..., OFF_POOL:OFF_CONF], row_len, w_pool, pool_scale)
    y_c = conformer_branch(p[..., OFF_CONF:OFF_GATE], grid, conf_w, conf_b, ln_w, ln_b, w_conf_out)
    g = jax.nn.sigmoid(p[..., OFF_GATE:].astype(jnp.float32)).astype(dtype)
    merged = (g[..., :D_MODEL] * y_a + g[..., D_MODEL:2 * D_MODEL] * y_b
              + g[..., 2 * D_MODEL:] * y_c)
    return merged @ w_out


def moe(t, w_router, router_bias, w_gate, w_up, w_down):
    f32 = jnp.float32
    n_tok = t.shape[0]
    scores = jax.nn.sigmoid(t.astype(f32) @ w_router.astype(f32))
    sel = scores + router_bias.astype(f32)
    grp_score = lax.top_k(sel.reshape(n_tok, N_EXPERT_GROUPS, EXPERTS_PER_GROUP), 2)[0].sum(-1)
    top_grp = lax.top_k(grp_score, TOPK_GROUPS)[1]
    grp_mask = jax.nn.one_hot(top_grp, N_EXPERT_GROUPS, dtype=f32).sum(1) > 0
    exp_mask = jnp.repeat(grp_mask, EXPERTS_PER_GROUP, axis=-1)
    top_e = lax.top_k(jnp.where(exp_mask, sel, -jnp.inf), TOP_K)[1]
    gw = jnp.take_along_axis(scores, top_e, axis=-1)
    gw = gw / jnp.sum(gw, axis=-1, keepdims=True)
    combine = jnp.sum(jax.nn.one_hot(top_e, N_EXPERTS, dtype=f32) * gw[..., None], axis=1).astype(t.dtype)
    out = jnp.zeros_like(t)
    for e in range(N_EXPERTS):
        hid = jax.nn.silu(t @ w_gate[e]) * (t @ w_up[e])
        out = out + combine[:, e:e + 1] * (hid @ w_down[e])
    return out


def setup_inputs(seed: int = 0) -> dict:
    key = jax.random.key(seed)
    ks = jax.random.split(key, 32)
    f32 = jnp.float32
    D, L = D_MODEL, DEPTH

    def nrm(k, shape, fan_in, gain=1.0):
        return jax.random.normal(k, shape, f32) * (gain * fan_in ** -0.5)

    def near_one(k, shape):
        return 1.0 + 0.05 * jax.random.normal(k, shape, f32)

    def small(k, shape):
        return 0.02 * jax.random.normal(k, shape, f32)

    dt0 = jnp.exp(jax.random.uniform(ks[12], (L, 2, SSD_HEADS), f32) * (np.log(0.1) - np.log(0.001))
                  + np.log(0.001))
    return {
        "x": jax.random.normal(ks[0], (BATCH, SEQ, D), f32),
        "c": jax.random.normal(ks[1], (BATCH, D), f32),
        "ctx": jax.random.normal(ks[2], (BATCH, CTX_LEN, D), f32),
        "c_ctx": jax.random.normal(ks[3], (D,), f32),
        "w_ada": nrm(ks[4], (L, D, 6 * D), D, 0.5),
        "b_ada": small(ks[5], (L, 6 * D)),
        "norm1_w": near_one(ks[6], (L, D)),
        "norm2_w": near_one(ks[7], (L, D)),
        "w_in": nrm(ks[8], (L, D, D_IN_PROJ), D),
        "ssd_conv_w": nrm(ks[9], (L, SSD_CONV, SSD_XBC), SSD_CONV),
        "ssd_conv_b": small(ks[10], (L, SSD_XBC)),
        "ssd_A_log": jnp.log(jax.random.uniform(ks[11], (L, 2, SSD_HEADS), f32, 1.0, 16.0)),
        "ssd_dt_bias": dt0 + jnp.log(-jnp.expm1(-dt0)),
        "ssd_D": near_one(ks[13], (L, 2, SSD_HEADS)),
        "ssd_norm_w": near_one(ks[14], (L, SSD_INNER)),
        "w_ssd_out": nrm(ks[15], (L, SSD_INNER, D), SSD_INNER),
        "w_pool": nrm(ks[16], (L, len(POOL_WINDOWS), POOL_GROUP, POOL_OUT), POOL_GROUP),
        "pool_scale": near_one(ks[17], (L, D)),
        "conf_conv_w": nrm(ks[18], (L, CONF_KERNEL, CONF_WIDTH), CONF_KERNEL),
        "conf_conv_b": small(ks[19], (L, CONF_WIDTH)),
        "conf_ln_w": near_one(ks[20], (L, CONF_WIDTH)),
        "conf_ln_b": small(ks[21], (L, CONF_WIDTH)),
        "w_conf_out": nrm(ks[22], (L, CONF_WIDTH, D), CONF_WIDTH),
        "w_out": nrm(ks[23], (L, D, D), D),
        "w_router": nrm(ks[24], (D, N_EXPERTS), D),
        "router_bias": 0.01 * jax.random.normal(ks[25], (N_EXPERTS,), f32),
        "w_exp_gate": nrm(ks[26], (L, N_EXPERTS, D, EXPERT_FF), D),
        "w_exp_up": nrm(ks[27], (L, N_EXPERTS, D, EXPERT_FF), D),
        "w_exp_down": nrm(ks[28], (L, N_EXPERTS, EXPERT_FF, D), EXPERT_FF),
        "final_norm_w": near_one(ks[29], (D,)),
    }


def reference(x, c, ctx, c_ctx, w_ada, b_ada, norm1_w, norm2_w, w_in, ssd_conv_w, ssd_conv_b,
              ssd_A_log, ssd_dt_bias, ssd_D, ssd_norm_w, w_ssd_out, w_pool, pool_scale,
              conf_conv_w, conf_conv_b, conf_ln_w, conf_ln_b, w_conf_out, w_out, w_router,
              router_bias, w_exp_gate, w_exp_up, w_exp_down, final_norm_w):
    bsz = x.shape[0]
    h0 = jnp.zeros((bsz, SSD_HEADS, SSD_HEAD_DIM, SSD_STATE), jnp.float32)
    for l in range(DEPTH):
        update_ctx = l < DEPTH - 1
        mod = (jax.nn.silu(c) @ w_ada[l] + b_ada[l])[:, None, :]
        mod_c = jax.nn.silu(c_ctx) @ w_ada[l] + b_ada[l]
        sh1, sc1, g1, sh2, sc2, g2 = jnp.split(mod, 6, axis=-1)
        sh1c, sc1c, g1c, sh2c, sc2c, g2c = jnp.split(mod_c, 6, axis=-1)

        h = rms_norm(x, norm1_w[l]) * (1 + sc1) + sh1
        hc = rms_norm(ctx, norm1_w[l]) * (1 + sc1c) + sh1c
        p = h @ w_in[l]
        pc = hc @ (w_in[l] if update_ctx else w_in[l][:, :OFF_Z])
        xs, bg, cg, dtr = ssd_inputs(p, ssd_conv_w[l], ssd_conv_b[l])
        xsc, bgc, cgc, dtrc = ssd_inputs(pc, ssd_conv_w[l], ssd_conv_b[l])
        y_lat = jnp.zeros(xs.shape, jnp.float32)
        y_ctx = jnp.zeros(xsc.shape, jnp.float32)
        for d in range(2):
            hs = slice(d * SSD_HEADS, (d + 1) * SSD_HEADS)
            dir_params = (ssd_A_log[l, d], ssd_dt_bias[l, d], ssd_D[l, d], d == 1)
            yc_d, s_ctx = ssd_direction(xsc, bgc, cgc, dtrc[..., hs], h0, *dir_params, update_ctx)
            yl_d, _ = ssd_direction(xs, bg, cg, dtr[..., hs], s_ctx, *dir_params, True)
            y_lat = y_lat + yl_d
            if update_ctx:
                y_ctx = y_ctx + yc_d
        branch_params = (ssd_norm_w[l], w_ssd_out[l], w_pool[l], pool_scale[l], conf_conv_w[l],
                         conf_conv_b[l], conf_ln_w[l], conf_ln_b[l], w_conf_out[l], w_out[l])
        x = x + g1 * merge_branches(p, y_lat.reshape(bsz, -1, SSD_INNER), True, *branch_params)

        h2 = rms_norm(x, norm2_w[l]) * (1 + sc2) + sh2
        moe_params = (w_router, router_bias, w_exp_gate[l], w_exp_up[l], w_exp_down[l])
        if update_ctx:
            ctx = ctx + g1c * merge_branches(pc, y_ctx.reshape(bsz, -1, SSD_INNER), False, *branch_params)
            h2c = rms_norm(ctx, norm2_w[l]) * (1 + sc2c) + sh2c
            n_lat = h2.shape[0] * h2.shape[1]
            tokens = jnp.concatenate([h2.reshape(-1, D_MODEL), h2c.reshape(-1, D_MODEL)], axis=0)
            out = moe(tokens, *moe_params)
            x = x + g2 * out[:n_lat].reshape(x.shape)
            ctx = ctx + g2c * out[n_lat:].reshape(ctx.shape)
        else:
            x = x + g2 * moe(h2.reshape(-1, D_MODEL), *moe_params).reshape(x.shape)
    return rms_norm(x, final_norm_w)
```

```python
import functools

import numpy as np
import jax
import jax.numpy as jnp
from jax import lax
from jax.experimental import pallas as pl
from jax.experimental.pallas import tpu as pltpu

GRID_W = 64
EPS = 1e-6
SSD_HEAD_DIM = 64
SSD_STATE = 128
SSD_CHUNK = 128
POOL_WINDOWS = (2, 4, 8, 16)
N_EXPERT_GROUPS = 4

LANES = 128
BF16_SUBLANES = 16
VMEM_LIMIT_BYTES = 56 * 1024 * 1024

ROW_TILE = 512
ELT_TILE = 256
MOE_TILE = 256
MOE_FF_SPLIT = 2

F32 = jnp.float32
BF16 = jnp.bfloat16


def _params(*sem):
    return pltpu.CompilerParams(dimension_semantics=sem, vmem_limit_bytes=VMEM_LIMIT_BYTES)


def _pick_tile(n, *offsets, choices=(1024, 512, 256, 128)):
    for t in choices:
        if n % t == 0 and all(o % t == 0 for o in offsets):
            return t
    raise ValueError(f"no lane tile divides {n} and offsets {offsets}")


def _dot(a, b):
    return jnp.dot(a, b, preferred_element_type=F32)


def _split_bf16(v, parts):
    out = []
    rem = v
    for _ in range(parts):
        hi = rem.astype(BF16)
        out.append(hi)
        rem = rem - hi.astype(F32)
    return out


def _silu(v):
    return v * jax.nn.sigmoid(v)


def _ada_kernel(c_ref, w_ref, b_ref, o_ref):
    s = _silu(c_ref[...]).astype(BF16)
    o_ref[...] = _dot(s, w_ref[...].astype(BF16)) + b_ref[...]


def _ada(c_rows, w, b):
    m, d = c_rows.shape
    n = w.shape[1]
    tn = _pick_tile(n, choices=(512, 256, 128))
    return pl.pallas_call(
        _ada_kernel,
        grid=(n // tn,),
        in_specs=[pl.BlockSpec((m, d), lambda j: (0, 0)),
                  pl.BlockSpec((d, tn), lambda j: (0, j)),
                  pl.BlockSpec((1, tn), lambda j: (0, j))],
        out_specs=pl.BlockSpec((m, tn), lambda j: (0, j)),
        out_shape=jax.ShapeDtypeStruct((m, n), F32),
        compiler_params=_params("arbitrary"),
        name="ada_mod",
    )(c_rows, w, b.reshape(1, n))


def _prenorm_math(x, w, sc, sh):
    ms = jnp.mean(x * x, axis=-1, keepdims=True)
    y = x * lax.rsqrt(ms + EPS) * w
    return y * (1.0 + sc) + sh


def _prenorm_kernel(x_ref, w_ref, sc_ref, sh_ref, o_ref):
    o_ref[...] = _prenorm_math(x_ref[...], w_ref[...], sc_ref[0], sh_ref[0]).astype(o_ref.dtype)


def _mod_spec(d, col, rows_per_mod, n_mod, tm):
    tiles_per_mod = rows_per_mod // tm
    return pl.BlockSpec((1, 1, d), lambda i: (jnp.minimum(i // tiles_per_mod, n_mod - 1), 0, col))


def _prenorm(xa, w, mods, sc_col, sh_col, seq, n_batch):
    t, d = xa.shape
    tm = ELT_TILE
    return pl.pallas_call(
        _prenorm_kernel,
        grid=(t // tm,),
        in_specs=[pl.BlockSpec((tm, d), lambda i: (i, 0)),
                  pl.BlockSpec((1, d), lambda i: (0, 0)),
                  _mod_spec(d, sc_col, seq, n_batch + 1, tm),
                  _mod_spec(d, sh_col, seq, n_batch + 1, tm)],
        out_specs=pl.BlockSpec((tm, d), lambda i: (i, 0)),
        out_shape=jax.ShapeDtypeStruct((t, d), BF16),
        compiler_params=_params("parallel"),
        name="prenorm",
    )(xa, w.reshape(1, d), mods, mods)


def _final_norm_kernel(x_ref, w_ref, o_ref):
    x = x_ref[...]
    ms = jnp.mean(x * x, axis=-1, keepdims=True)
    o_ref[...] = x * lax.rsqrt(ms + EPS) * w_ref[...]


def _final_norm(xa, w, t_out):
    d = xa.shape[1]
    tm = ELT_TILE
    return pl.pallas_call(
        _final_norm_kernel,
        grid=(t_out // tm,),
        in_specs=[pl.BlockSpec((tm, d), lambda i: (i, 0)),
                  pl.BlockSpec((1, d), lambda i: (0, 0))],
        out_specs=pl.BlockSpec((tm, d), lambda i: (i, 0)),
        out_shape=jax.ShapeDtypeStruct((t_out, d), F32),
        compiler_params=_params("parallel"),
        name="final_norm",
    )(xa, w.reshape(1, d))


def _mm_plain_kernel(a_ref, w_ref, o_ref):
    o_ref[...] = _dot(a_ref[...], w_ref[...]).astype(o_ref.dtype)


def _mm_gate_kernel(a_ref, w_ref, g_ref, o_ref):
    acc = _dot(a_ref[...], w_ref[...])
    o_ref[...] = (jax.nn.sigmoid(g_ref[...].astype(F32)) * acc).astype(o_ref.dtype)


def _mm_gate_acc_kernel(a_ref, w_ref, g_ref, prev_ref, o_ref):
    acc = _dot(a_ref[...], w_ref[...])
    o_ref[...] = (prev_ref[...].astype(F32)
                  + jax.nn.sigmoid(g_ref[...].astype(F32)) * acc).astype(o_ref.dtype)


def _mm_resid_kernel(a_ref, w_ref, x_ref, gm_ref, o_ref):
    acc = _dot(a_ref[...], w_ref[...])
    o_ref[...] = x_ref[...] + gm_ref[0] * acc


def _mm(a, w, out_dtype, *, gate=None, gate_off=0, prev=None, resid=None, mods=None,
        mod_col=0, seq=0, n_batch=0, name="mm"):
    t, k = a.shape
    n = w.shape[1]
    tm = ROW_TILE
    tn = _pick_tile(n, gate_off)
    grid = (n // tn, t // tm)
    a_spec = pl.BlockSpec((tm, k), lambda j, i: (i, 0))
    w_spec = pl.BlockSpec((k, tn), lambda j, i: (0, j))
    o_spec = pl.BlockSpec((tm, tn), lambda j, i: (i, j))
    goff = gate_off // tn
    g_spec = pl.BlockSpec((tm, tn), lambda j, i: (i, goff + j))
    aliases = {}
    if resid is not None:
        tiles_per_mod = seq // tm
        mcol = mod_col * (n // tn)
        gm_spec = pl.BlockSpec(
            (1, 1, tn), lambda j, i: (jnp.minimum(i // tiles_per_mod, n_batch), 0, mcol + j))
        kern, ins, specs = _mm_resid_kernel, (a, w, resid, mods), [a_spec, w_spec, o_spec, gm_spec]
        aliases = {2: 0}
    elif prev is not None:
        kern, ins, specs = _mm_gate_acc_kernel, (a, w, gate, prev), [a_spec, w_spec, g_spec, o_spec]
        aliases = {3: 0}
    elif gate is not None:
        kern, ins, specs = _mm_gate_kernel, (a, w, gate), [a_spec, w_spec, g_spec]
    else:
        kern, ins, specs = _mm_plain_kernel, (a, w), [a_spec, w_spec]
    return pl.pallas_call(
        kern,
        grid=grid,
        in_specs=specs,
        out_specs=o_spec,
        out_shape=jax.ShapeDtypeStruct((t, n), out_dtype),
        input_output_aliases=aliases,
        compiler_params=_params("parallel", "arbitrary"),
        name=name,
    )(*ins)


def _seqconv_kernel(prev_ref, cur_ref, next_ref, w_ref, b_ref, o_ref, scr, *, ts, taps):
    i = pl.program_id(1)
    n = pl.num_programs(1)
    h = BF16_SUBLANES
    scr[pl.ds(0, h), :] = jnp.where(i > 0, prev_ref[...].astype(F32), 0.0)
    scr[pl.ds(h, ts), :] = cur_ref[...].astype(F32)
    scr[pl.ds(h + ts, h), :] = jnp.where(i < n - 1, next_ref[...].astype(F32), 0.0)
    acc = jnp.broadcast_to(b_ref[...], (ts, b_ref.shape[1]))
    for k in range(taps):
        acc = acc + w_ref[k:k + 1, :] * scr[pl.ds(h - taps // 2 + k, ts), :]
    o_ref[...] = _silu(acc).astype(o_ref.dtype)


def _seqconv(p, w, b, row0, seq, n_seq):
    taps, width = w.shape
    ts = min(seq, 512)
    tc = _pick_tile(width, choices=(512, 256, 128))
    h = BF16_SUBLANES
    tiles = seq // ts
    cur0 = row0 // ts
    halo0 = row0 // h
    per_tile = ts // h
    last_halo = (row0 + n_seq * seq) // h - 1

    def prev_map(s, i, j):
        return (jnp.maximum(halo0 + (s * tiles + i) * per_tile - 1, 0), j)

    def next_map(s, i, j):
        return (jnp.minimum(halo0 + (s * tiles + i + 1) * per_tile, last_halo), j)

    return pl.pallas_call(
        functools.partial(_seqconv_kernel, ts=ts, taps=taps),
        grid=(n_seq, tiles, width // tc),
        in_specs=[pl.BlockSpec((h, tc), prev_map),
                  pl.BlockSpec((ts, tc), lambda s, i, j: (cur0 + s * tiles + i, j)),
                  pl.BlockSpec((h, tc), next_map),
                  pl.BlockSpec((taps, tc), lambda s, i, j: (0, j)),
                  pl.BlockSpec((1, tc), lambda s, i, j: (0, j))],
        out_specs=pl.BlockSpec((ts, tc), lambda s, i, j: (s * tiles + i, j)),
        out_shape=jax.ShapeDtypeStruct((n_seq * seq, width), BF16),
        scratch_shapes=[pltpu.VMEM((ts + 2 * h, tc), F32)],
        compiler_params=_params("parallel", "parallel", "parallel"),
        name="ssd_seqconv",
    )(p, p, p, w, b.reshape(1, width))


def _ssd_kernel(xbc_ref, dt_ref, alog_ref, dtb_ref, dsk_ref, tri_ref, exp_ref, h0_ref,
                y_ref, hfin_ref, st, *, inner, n_groups, n_state, heads_per_group, head_dim):
    c = pl.program_id(2)
    q = dt_ref.shape[0]
    rp = heads_per_group * head_dim
    gn = n_groups * n_state

    @pl.when(c == 0)
    def _():
        st[...] = h0_ref[0, 0]

    x = dt_ref[...] + dtb_ref[0]
    dt = jnp.maximum(x, 0.0) + jnp.log1p(jnp.exp(-jnp.abs(x)))
    a = dt * (-jnp.exp(alog_ref[0]))
    tri = tri_ref[0]
    tri_b = tri.astype(BF16)
    acs = sum(_dot(tri_b, part) for part in _split_bf16(a, 3))
    atot = jnp.sum(a, axis=0, keepdims=True)
    acs_t = acs.T
    stack = jnp.concatenate(
        [dt, jnp.exp(acs), jnp.exp(atot - acs), jnp.broadcast_to(jnp.exp(atot), (8, LANES))], axis=0)
    expand = exp_ref[...]
    wide = sum(_dot(part, expand) for part in _split_bf16(stack, 2))
    dt_x = wide[0:q]
    eacs_x = wide[q:2 * q]
    edec_x = wide[2 * q:3 * q]
    etot_x = wide[3 * q:3 * q + 1]

    xs = xbc_ref[:, 0:inner].astype(F32)
    xdt = xs * dt_x
    xdt_b = xdt.astype(BF16)
    xw_b = (xdt * edec_x).astype(BF16)
    mask = tri > 0.0
    lane_head = lax.broadcasted_iota(jnp.int32, (1, rp), 1) // head_dim
    dsk = dsk_ref[0]

    for g in range(n_groups):
        bg = xbc_ref[:, inner + g * n_state: inner + (g + 1) * n_state]
        cg = xbc_ref[:, inner + gn + g * n_state: inner + gn + (g + 1) * n_state]
        bg_t = bg.astype(F32).T.astype(BF16)
        cb = _dot(cg, bg_t)
        sl = slice(g * rp, (g + 1) * rp)
        chunk_state = _dot(bg_t, xw_b[:, sl])
        st_g = st[g]
        y_off = _dot(cg, st_g.astype(BF16)) * eacs_x[:, sl]
        ms, xm = [], []
        for r in range(heads_per_group):
            hh = g * heads_per_group + r
            seg = acs[:, hh:hh + 1] - acs_t[hh:hh + 1, :]
            decay = jnp.exp(jnp.where(mask, seg, -jnp.inf))
            ms.append((cb * decay).astype(BF16))
            xm.append(jnp.where(lane_head == r, xdt_b[:, sl], jnp.zeros_like(xdt_b[:, sl])))
        y_diag = _dot(jnp.concatenate(ms, axis=1), jnp.concatenate(xm, axis=0))
        y = y_diag + y_off + dsk[:, sl] * xs[:, sl]
        y_ref[0, :, sl] = y.astype(y_ref.dtype)
        st[g] = st_g * etot_x[:, sl] + chunk_state

    @pl.when(c == pl.num_programs(2) - 1)
    def _():
        hfin_ref[0, 0] = st[...]


def _ssd(xbc, dtr, dt_row0, a_log, dt_bias, d_skip, h0, seq, n_seq, geo):
    inner, n_groups, n_state, hpg, head_dim = geo
    heads = n_groups * hpg
    q = SSD_CHUNK
    nc = seq // q
    rp = hpg * head_dim
    xbc_w = xbc.shape[1]

    def pad_heads(v):
        return jnp.pad(v.astype(F32), ((0, 0), (0, LANES - heads))).reshape(2, 1, LANES)

    dsk = jnp.repeat(d_skip.astype(F32), head_dim, axis=-1).reshape(2, 1, inner)
    low = np.tril(np.ones((q, q), np.float32))
    tri = jnp.asarray(np.stack([low, low.T]))
    expand = np.zeros((LANES, inner), np.float32)
    for hh in range(heads):
        expand[hh, hh * head_dim:(hh + 1) * head_dim] = 1.0
    expand = jnp.asarray(expand, dtype=BF16)
    dt_blk0 = dt_row0 // q

    def chunk(s, d, c):
        return s * nc + c + d * (nc - 1 - 2 * c)

    kern = functools.partial(_ssd_kernel, inner=inner, n_groups=n_groups, n_state=n_state,
                             heads_per_group=hpg, head_dim=head_dim)
    y, hfin = pl.pallas_call(
        kern,
        grid=(n_seq, 2, nc),
        in_specs=[pl.BlockSpec((q, xbc_w), lambda s, d, c: (chunk(s, d, c), 0)),
                  pl.BlockSpec((q, LANES), lambda s, d, c: (dt_blk0 + chunk(s, d, c), d)),
                  pl.BlockSpec((1, 1, LANES), lambda s, d, c: (d, 0, 0)),
                  pl.BlockSpec((1, 1, LANES), lambda s, d, c: (d, 0, 0)),
                  pl.BlockSpec((1, 1, inner), lambda s, d, c: (d, 0, 0)),
                  pl.BlockSpec((1, q, q), lambda s, d, c: (d, 0, 0)),
                  pl.BlockSpec((LANES, inner), lambda s, d, c: (0, 0)),
                  pl.BlockSpec((1, 1, n_groups, n_state, rp), lambda s, d, c: (s, d, 0, 0, 0))],
        out_specs=[pl.BlockSpec((1, q, inner), lambda s, d, c: (d, chunk(s, d, c), 0)),
                   pl.BlockSpec((1, 1, n_groups, n_state, rp), lambda s, d, c: (s, d, 0, 0, 0))],
        out_shape=[jax.ShapeDtypeStruct((2, n_seq * seq, inner), BF16),
                   jax.ShapeDtypeStruct((n_seq, 2, n_groups, n_state, rp), F32)],
        scratch_shapes=[pltpu.VMEM((n_groups, n_state, rp), F32)],
        compiler_params=_params("parallel", "parallel", "arbitrary"),
        name="ssd_scan",
    )(xbc, dtr, pad_heads(a_log), pad_heads(dt_bias), dsk, tri, expand, h0)
    return y, hfin


def _gate_norm_kernel(y_ref, z_ref, w_ref, o_ref):
    v = (y_ref[0].astype(F32) + y_ref[1].astype(F32)) * _silu(z_ref[...].astype(F32))
    ms = jnp.mean(v * v, axis=-1, keepdims=True)
    o_ref[...] = (v * lax.rsqrt(ms + EPS) * w_ref[...]).astype(o_ref.dtype)


def _gate_norm(y, p, z_off, w):
    _, t, inner = y.shape
    tm = ELT_TILE
    zb = z_off // inner
    return pl.pallas_call(
        _gate_norm_kernel,
        grid=(t // tm,),
        in_specs=[pl.BlockSpec((2, tm, inner), lambda i: (0, i, 0)),
                  pl.BlockSpec((tm, inner), lambda i: (i, zb)),
                  pl.BlockSpec((1, inner), lambda i: (0, 0))],
        out_specs=pl.BlockSpec((tm, inner), lambda i: (i, 0)),
        out_shape=jax.ShapeDtypeStruct((t, inner), BF16),
        compiler_params=_params("parallel"),
        name="ssd_gate_norm",
    )(y, p, w.reshape(1, inner))


def _pool_kernel(u_ref, band_ref, invc_ref, wp_ref, ps_ref, g_ref, prev_ref, o_ref):
    u = u_ref[...]
    win_sum = _dot(band_ref[0], u)
    m = win_sum * invc_ref[0][:, 0:1] - u.astype(F32)
    yb = _dot(m.astype(BF16), wp_ref[0]) * ps_ref[...]
    o_ref[...] = (prev_ref[...].astype(F32)
                  + jax.nn.sigmoid(g_ref[...].astype(F32)) * yb).astype(o_ref.dtype)


def _pool_tables(tp, row_len):
    nw = len(POOL_WINDOWS)
    band = np.zeros((nw, tp, tp), np.float32)
    invc = np.zeros((nw, tp, LANES), np.float32)
    j = np.arange(row_len)
    for k, w in enumerate(POOL_WINDOWS):
        lo = np.clip(j - w // 2, 0, row_len - 1)
        hi = np.clip(j + w // 2 - 1, 0, row_len - 1)
        for r0 in range(0, tp, row_len):
            for jj in range(row_len):
                band[k, r0 + jj, r0 + lo[jj]: r0 + hi[jj] + 1] = 1.0
                invc[k, r0 + jj, :] = 1.0 / (hi[jj] - lo[jj] + 1)
    return jnp.asarray(band, dtype=BF16), jnp.asarray(invc)


def _pool(p, merged, w_pool, pool_scale, u_off, gate_off, row0, n_rows, row_len):
    nw, pg, po = w_pool.shape
    d = merged.shape[1]
    tp = max(row_len, min(512, n_rows))
    band, invc = _pool_tables(tp, row_len)
    ub = u_off // pg
    gb = gate_off // po
    r0 = row0 // tp
    return pl.pallas_call(
        _pool_kernel,
        grid=(n_rows // tp, nw),
        in_specs=[pl.BlockSpec((tp, pg), lambda i, k: (r0 + i, ub + k)),
                  pl.BlockSpec((1, tp, tp), lambda i, k: (k, 0, 0)),
                  pl.BlockSpec((1, tp, LANES), lambda i, k: (k, 0, 0)),
                  pl.BlockSpec((1, pg, po), lambda i, k: (k, 0, 0)),
                  pl.BlockSpec((1, po), lambda i, k: (0, k)),
                  pl.BlockSpec((tp, po), lambda i, k: (r0 + i, gb + k)),
                  pl.BlockSpec((tp, po), lambda i, k: (r0 + i, k))],
        out_specs=pl.BlockSpec((tp, po), lambda i, k: (r0 + i, k)),
        out_shape=jax.ShapeDtypeStruct(merged.shape, merged.dtype),
        input_output_aliases={6: 0},
        compiler_params=_params("parallel", "arbitrary"),
        name="pool_branch",
    )(p, band, invc, w_pool, pool_scale.reshape(1, d), p, merged)


def _confconv_kernel(a_ref, b_ref, w_ref, bias_ref, o_ref, scr, *, seq, taps, stride, chunk, pad):
    lanes = a_ref.shape[1]
    first = pad - (taps // 2) * stride
    scr[pl.ds(0, pad), :] = jnp.zeros((pad, lanes), F32)
    scr[pl.ds(pad + seq, pad), :] = jnp.zeros((pad, lanes), F32)
    scr[pl.ds(pad, seq), :] = a_ref[...].astype(F32) * jax.nn.sigmoid(b_ref[...].astype(F32))

    def do_chunk(base):
        acc = jnp.broadcast_to(bias_ref[...], (chunk, lanes))
        for k in range(taps):
            acc = acc + w_ref[k:k + 1, :] * scr[pl.ds(base + first + k * stride, chunk), :]
        o_ref[pl.ds(base, chunk), :] = acc

    n_chunks = seq // chunk
    if n_chunks == 1:
        do_chunk(0)
    else:
        def body(ci, carry):
            do_chunk(pl.multiple_of(ci * chunk, chunk))
            return carry
        lax.fori_loop(0, n_chunks, body, 0)


def _confconv(p, w, b, a_off, row0, seq, n_seq, stride):
    taps, cw = w.shape
    tc = LANES
    chunk = min(seq, 256)
    pad_alloc = -(-((taps // 2) * stride) // 8) * 8
    ab = a_off // tc
    bb = (a_off + cw) // tc
    r0 = row0 // seq
    kern = functools.partial(_confconv_kernel, seq=seq, taps=taps, stride=stride, chunk=chunk,
                             pad=pad_alloc)
    return pl.pallas_call(
        kern,
        grid=(n_seq, cw // tc),
        in_specs=[pl.BlockSpec((seq, tc), lambda s, j: (r0 + s, ab + j)),
                  pl.BlockSpec((seq, tc), lambda s, j: (r0 + s, bb + j)),
                  pl.BlockSpec((taps, tc), lambda s, j: (0, j)),
                  pl.BlockSpec((1, tc), lambda s, j: (0, j))],
        out_specs=pl.BlockSpec((seq, tc), lambda s, j: (s, j)),
        out_shape=jax.ShapeDtypeStruct((n_seq * seq, cw), F32),
        scratch_shapes=[pltpu.VMEM((seq + 2 * pad_alloc, tc), F32)],
        compiler_params=_params("parallel", "parallel"),
        name="conf_conv",
    )(p, p, w, b.reshape(1, cw))


def _ln_silu_kernel(x_ref, w_ref, b_ref, o_ref):
    x = x_ref[...]
    mu = jnp.mean(x, axis=-1, keepdims=True)
    xc = x - mu
    var = jnp.mean(xc * xc, axis=-1, keepdims=True)
    y = xc * lax.rsqrt(var + EPS) * w_ref[...] + b_ref[...]
    o_ref[...] = _silu(y).astype(o_ref.dtype)


def _ln_silu(v, w, b):
    t, cw = v.shape
    tm = ELT_TILE
    return pl.pallas_call(
        _ln_silu_kernel,
        grid=(t // tm,),
        in_specs=[pl.BlockSpec((tm, cw), lambda i: (i, 0)),
                  pl.BlockSpec((1, cw), lambda i: (0, 0)),
                  pl.BlockSpec((1, cw), lambda i: (0, 0))],
        out_specs=pl.BlockSpec((tm, cw), lambda i: (i, 0)),
        out_shape=jax.ShapeDtypeStruct((t, cw), BF16),
        compiler_params=_params("parallel"),
        name="conf_ln_silu",
    )(v, w.reshape(1, cw), b.reshape(1, cw))


def _top1(rows):
    best, idx = rows[0], jnp.zeros(rows[0].shape, jnp.int32)
    for j in range(1, len(rows)):
        better = rows[j] > best
        idx = jnp.where(better, j, idx)
        best = jnp.where(better, rows[j], best)
    return best, idx


def _router_kernel(x_ref, w_ref, sc_ref, sh_ref, wr_ref, rb_ref, h_ref, sel_ref, *, n_experts):
    h2 = _prenorm_math(x_ref[...], w_ref[...], sc_ref[0], sh_ref[0])
    h_ref[...] = h2
    h_hi, h_lo = _split_bf16(h2, 2)
    w_hi, w_lo = _split_bf16(wr_ref[...], 2)
    nt = (((1,), (1,)), ((), ()))

    def dg(a, b):
        return lax.dot_general(a, b, nt, preferred_element_type=F32)

    logits = dg(w_hi, h_hi) + dg(w_hi, h_lo) + dg(w_lo, h_hi)
    scores = jax.nn.sigmoid(logits)
    sel = scores + rb_ref[:, 0:1]
    per_group = n_experts // N_EXPERT_GROUPS
    sel_rows = [sel[e:e + 1, :] for e in range(n_experts)]
    score_rows = [scores[e:e + 1, :] for e in range(n_experts)]
    neg = jnp.full(sel_rows[0].shape, -jnp.inf, F32)

    group_scores = []
    for g in range(N_EXPERT_GROUPS):
        rows = sel_rows[g * per_group:(g + 1) * per_group]
        m1, i1 = _top1(rows)
        m2, _ = _top1([jnp.where(i1 == j, neg, rows[j]) for j in range(per_group)])
        group_scores.append(m1 + m2)
    _, best_group = _top1(group_scores)

    masked = [jnp.where(best_group == e // per_group, sel_rows[e], neg) for e in range(n_experts)]
    _, e1 = _top1(masked)
    _, e2 = _top1([jnp.where(e1 == e, neg, masked[e]) for e in range(n_experts)])
    s1 = sum(jnp.where(e1 == e, score_rows[e], 0.0) for e in range(n_experts))
    s2 = sum(jnp.where(e2 == e, score_rows[e], 0.0) for e in range(n_experts))
    tot = s1 + s2
    sel_ref[...] = jnp.zeros(sel_ref.shape, F32)
    sel_ref[0:1, :] = e1.astype(F32)
    sel_ref[1:2, :] = e2.astype(F32)
    sel_ref[2:3, :] = s1 / tot
    sel_ref[3:4, :] = s2 / tot


def _router(xa, w, mods, sc_col, sh_col, seq, n_batch, w_router, router_bias):
    t, d = xa.shape
    n_experts = w_router.shape[1]
    tm = ELT_TILE
    wr_t = w_router.astype(F32).T
    rb = jnp.broadcast_to(router_bias.astype(F32)[:, None], (n_experts, LANES))
    return pl.pallas_call(
        functools.partial(_router_kernel, n_experts=n_experts),
        grid=(t // tm,),
        in_specs=[pl.BlockSpec((tm, d), lambda i: (i, 0)),
                  pl.BlockSpec((1, d), lambda i: (0, 0)),
                  _mod_spec(d, sc_col, seq, n_batch + 1, tm),
                  _mod_spec(d, sh_col, seq, n_batch + 1, tm),
                  pl.BlockSpec((n_experts, d), lambda i: (0, 0)),
                  pl.BlockSpec((n_experts, LANES), lambda i: (0, 0))],
        out_specs=[pl.BlockSpec((tm, d), lambda i: (i, 0)),
                   pl.BlockSpec((8, tm), lambda i: (0, i))],
        out_shape=[jax.ShapeDtypeStruct((t, d), F32),
                   jax.ShapeDtypeStruct((8, t), F32)],
        compiler_params=_params("parallel"),
        name="moe_prenorm_router",
    )(xa, w.reshape(1, d), mods, mods, wr_t, rb)


def _row_copy(src_hbm, row, dst, slot, sem):
    return pltpu.make_async_copy(src_hbm.at[pl.ds(row, 1), :], dst.at[pl.ds(slot, 1), :], sem)


def _moe_kernel(te_ref, nv_ref, tok_ref, gw_ref, h_hbm, wg_ref, wu_ref, wd_ref, o_ref,
                xbuf, acc, sem, *, tm):
    i = pl.program_id(0)
    f = pl.program_id(1)
    valid = i < nv_ref[0]

    @pl.when(jnp.logical_and(valid, f == 0))
    def _():
        def issue(r, carry):
            _row_copy(h_hbm, tok_ref[0, 0, r], xbuf, r, sem).start()
            return carry
        lax.fori_loop(0, tm, issue, 0)

        def drain(r, carry):
            _row_copy(h_hbm, 0, xbuf, r, sem).wait()
            return carry
        lax.fori_loop(0, tm, drain, 0)

    @pl.when(valid)
    def _():
        x = xbuf[...].astype(BF16)
        hid = _silu(_dot(x, wg_ref[0])) * _dot(x, wu_ref[0]) * gw_ref[:, 0:1]
        part = _dot(hid.astype(BF16), wd_ref[0])

        @pl.when(f == 0)
        def _():
            acc[...] = part

        @pl.when(f > 0)
        def _():
            acc[...] += part

    @pl.when(f == pl.num_programs(1) - 1)
    def _():
        @pl.when(valid)
        def _():
            o_ref[...] = acc[...]

        @pl.when(jnp.logical_not(valid))
        def _():
            o_ref[...] = jnp.zeros(o_ref.shape, o_ref.dtype)


def _moe(h2, tile_expert, n_valid, row_token, row_gate, wg, wu, wd):
    t, d = h2.shape
    n_experts, _, ff = wg.shape
    tm = MOE_TILE
    n_tiles = tile_expert.shape[0]
    ffh = ff // MOE_FF_SPLIT
    grid_spec = pltpu.PrefetchScalarGridSpec(
        num_scalar_prefetch=2,
        grid=(n_tiles, MOE_FF_SPLIT),
        in_specs=[pl.BlockSpec((1, 1, tm), lambda i, f, te, nv: (i, 0, 0), memory_space=pltpu.SMEM),
                  pl.BlockSpec((tm, LANES), lambda i, f, te, nv: (i, 0)),
                  pl.BlockSpec(memory_space=pl.ANY),
                  pl.BlockSpec((1, d, ffh), lambda i, f, te, nv: (te[i], 0, f)),
                  pl.BlockSpec((1, d, ffh), lambda i, f, te, nv: (te[i], 0, f)),
                  pl.BlockSpec((1, ffh, d), lambda i, f, te, nv: (te[i], f, 0))],
        out_specs=pl.BlockSpec((tm, d), lambda i, f, te, nv: (i, 0)),
        scratch_shapes=[pltpu.VMEM((tm, d), F32), pltpu.VMEM((tm, d), F32),
                        pltpu.SemaphoreType.DMA(())],
    )
    return pl.pallas_call(
        functools.partial(_moe_kernel, tm=tm),
        grid_spec=grid_spec,
        out_shape=jax.ShapeDtypeStruct((n_tiles * tm, d), F32),
        compiler_params=_params("arbitrary", "arbitrary"),
        name="moe_experts",
    )(tile_expert, n_valid, row_token.reshape(n_tiles, 1, tm), row_gate, h2, wg, wu, wd)


def _combine_kernel(p0_ref, p1_ref, x_ref, gm_ref, ys_hbm, o_ref, buf, sem, *, tc):
    def issue(r, carry):
        _row_copy(ys_hbm, p0_ref[0, 0, r], buf.at[0], r, sem).start()
        _row_copy(ys_hbm, p1_ref[0, 0, r], buf.at[1], r, sem).start()
        return carry
    lax.fori_loop(0, tc, issue, 0)

    def drain(r, carry):
        _row_copy(ys_hbm, 0, buf.at[0], r, sem).wait()
        _row_copy(ys_hbm, 0, buf.at[1], r, sem).wait()
        return carry
    lax.fori_loop(0, tc, drain, 0)
    o_ref[...] = x_ref[...] + gm_ref[0] * (buf[0] + buf[1])


def _combine(xa, ys, pos0, pos1, mods, mod_col, seq, n_batch):
    t, d = xa.shape
    tc = ELT_TILE
    n = t // tc
    smem = functools.partial(pl.BlockSpec, (1, 1, tc), lambda i: (i, 0, 0), memory_space=pltpu.SMEM)
    return pl.pallas_call(
        functools.partial(_combine_kernel, tc=tc),
        grid=(n,),
        in_specs=[smem(), smem(),
                  pl.BlockSpec((tc, d), lambda i: (i, 0)),
                  _mod_spec(d, mod_col, seq, n_batch + 1, tc),
                  pl.BlockSpec(memory_space=pl.ANY)],
        out_specs=pl.BlockSpec((tc, d), lambda i: (i, 0)),
        out_shape=jax.ShapeDtypeStruct((t, d), F32),
        scratch_shapes=[pltpu.VMEM((2, tc, d), F32), pltpu.SemaphoreType.DMA(())],
        input_output_aliases={2: 0},
        compiler_params=_params("arbitrary"),
        name="moe_combine",
    )(pos0.reshape(n, 1, tc), pos1.reshape(n, 1, tc), xa, mods, ys)


def _routing_tables(sel, n_experts, tm):
    t = sel.shape[1]
    e = sel[0:2].astype(jnp.int32).reshape(-1)
    gate = sel[2:4].reshape(-1)
    onehot = (e[:, None] == jnp.arange(n_experts, dtype=jnp.int32)[None, :]).astype(jnp.int32)
    incl = jnp.cumsum(onehot, axis=0)
    rank = jnp.sum((incl - onehot) * onehot, axis=1)
    counts = incl[-1]
    tiles = (counts + tm - 1) // tm
    tile_end = jnp.cumsum(tiles)
    tile_start = tile_end - tiles
    pos = tile_start[e] * tm + rank
    n_tiles = (2 * t) // tm + n_experts
    token = jnp.tile(jnp.arange(t, dtype=jnp.int32), 2)
    row_token = jnp.zeros((n_tiles * tm,), jnp.int32).at[pos].set(token)
    row_gate = jnp.zeros((n_tiles * tm,), F32).at[pos].set(gate)
    row_gate = jnp.broadcast_to(row_gate[:, None], (n_tiles * tm, LANES))
    n_valid = tile_end[-1]
    tile_ids = jnp.minimum(jnp.arange(n_tiles, dtype=jnp.int32), n_valid - 1)
    tile_expert = jnp.sum((tile_ids[:, None] >= tile_end[None, :]).astype(jnp.int32), axis=1)
    return tile_expert, n_valid.reshape(1), row_token, row_gate, pos[:t], pos[t:]


def kernel(x, c, ctx, c_ctx, w_ada, b_ada, norm1_w, norm2_w, w_in, ssd_conv_w, ssd_conv_b,
           ssd_A_log, ssd_dt_bias, ssd_D, ssd_norm_w, w_ssd_out, w_pool, pool_scale,
           conf_conv_w, conf_conv_b, conf_ln_w, conf_ln_b, w_conf_out, w_out, w_router,
           router_bias, w_exp_gate, w_exp_up, w_exp_down, final_norm_w):
    n_batch, seq, d = x.shape
    ctx_len = ctx.shape[1]
    depth = w_in.shape[0]
    inner = w_ssd_out.shape[1]
    heads = ssd_A_log.shape[-1]
    xbc_w = ssd_conv_w.shape[-1]
    gn = (xbc_w - inner) // 2
    n_groups = gn // SSD_STATE
    hpg = heads // n_groups
    geo = (inner, n_groups, SSD_STATE, hpg, SSD_HEAD_DIM)
    pool_w = w_pool.shape[1] * w_pool.shape[2]
    conf_w = conf_conv_w.shape[-1]
    n_experts = w_router.shape[1]
    assert heads <= LANES and hpg * SSD_HEAD_DIM * n_groups == inner
    assert seq % GRID_W == 0 and seq % ROW_TILE == 0 and ctx_len % ELT_TILE == 0

    off_dt = xbc_w
    off_z = off_dt + 2 * heads
    off_pool = off_z + inner
    off_conf = off_pool + pool_w
    off_gate = off_conf + 2 * conf_w
    p_z = xbc_w
    p_pool = p_z + inner
    p_conf = p_pool + pool_w
    p_gate = p_conf + 2 * conf_w

    t_lat = n_batch * seq
    t_ctx = n_batch * ctx_len
    xa = jnp.concatenate([x.reshape(t_lat, d), ctx.reshape(t_ctx, d)], axis=0)
    t_all = t_lat + t_ctx
    assert t_all % ROW_TILE == 0 and ctx_len * n_batch <= seq

    c_rows = jnp.zeros((BF16_SUBLANES, d), F32).at[:n_batch].set(c).at[n_batch].set(c_ctx)
    h0 = jnp.zeros((n_batch, 2, n_groups, SSD_STATE, hpg * SSD_HEAD_DIM), F32)

    for l in range(depth):
        mods = _ada(c_rows, w_ada[l], b_ada[l]).reshape(BF16_SUBLANES, 1, 6 * d)
        w_l = w_in[l]
        w_main = jnp.concatenate([w_l[:, :off_dt], w_l[:, off_z:]], axis=1).astype(BF16)
        w_dt = jnp.zeros((d, 2 * LANES), F32)
        w_dt = w_dt.at[:, :heads].set(w_l[:, off_dt:off_dt + heads])
        w_dt = w_dt.at[:, LANES:LANES + heads].set(w_l[:, off_dt + heads:off_z]).astype(BF16)

        h = _prenorm(xa, norm1_w[l], mods, 1, 0, seq, n_batch)
        p = _mm(h, w_main, BF16, name="in_proj")
        dtr = _mm(h, w_dt, F32, name="dt_proj")

        xbc_ctx = _seqconv(p, ssd_conv_w[l], ssd_conv_b[l], t_lat, ctx_len, n_batch)
        xbc_lat = _seqconv(p, ssd_conv_w[l], ssd_conv_b[l], 0, seq, n_batch)
        y_ctx, s_ctx = _ssd(xbc_ctx, dtr, t_lat, ssd_A_log[l], ssd_dt_bias[l], ssd_D[l], h0,
                            ctx_len, n_batch, geo)
        y_lat, _ = _ssd(xbc_lat, dtr, 0, ssd_A_log[l], ssd_dt_bias[l], ssd_D[l], s_ctx,
                        seq, n_batch, geo)
        y_all = jnp.concatenate([y_lat, y_ctx], axis=1)
        ya = _gate_norm(y_all, p, p_z, ssd_norm_w[l])
        merged = _mm(ya, w_ssd_out[l].astype(BF16), BF16, gate=p, gate_off=p_gate, name="ssd_out")

        wp = w_pool[l].astype(BF16)
        merged = _pool(p, merged, wp, pool_scale[l], p_pool, p_gate + d, 0, t_lat, GRID_W)
        merged = _pool(p, merged, wp, pool_scale[l], p_pool, p_gate + d, t_lat, t_ctx, ctx_len)

        v_lat = _confconv(p, conf_conv_w[l], conf_conv_b[l], p_conf, 0, seq, n_batch, GRID_W)
        v_ctx = _confconv(p, conf_conv_w[l], conf_conv_b[l], p_conf, t_lat, ctx_len, n_batch, 1)
        ca = _ln_silu(jnp.concatenate([v_lat, v_ctx], axis=0), conf_ln_w[l], conf_ln_b[l])
        merged = _mm(ca, w_conf_out[l].astype(BF16), BF16, gate=p, gate_off=p_gate + 2 * d,
                     prev=merged, name="conf_out")

        xa = _mm(merged, w_out[l].astype(BF16), F32, resid=xa, mods=mods, mod_col=2,
                 seq=seq, n_batch=n_batch, name="mixer_out")

        h2, sel = _router(xa, norm2_w[l], mods, 4, 3, seq, n_batch, w_router, router_bias)
        tile_expert, n_valid, row_token, row_gate, pos0, pos1 = _routing_tables(
            sel, n_experts, MOE_TILE)
        ys = _moe(h2, tile_expert, n_valid, row_token, row_gate,
                  w_exp_gate[l].astype(BF16), w_exp_up[l].astype(BF16), w_exp_down[l].astype(BF16))
        xa = _combine(xa, ys, pos0, pos1, mods, 5, seq, n_batch)

    return _final_norm(xa, final_norm_w, t_lat).reshape(n_batch, seq, d)
```

```python
import functools

import numpy as np
import jax
import jax.numpy as jnp
from jax import lax
from jax.experimental import pallas as pl
from jax.experimental.pallas import tpu as pltpu

GRID_W = 64
EPS = 1e-6
SSD_HEAD_DIM = 64
SSD_STATE = 128
SSD_CHUNK = 128
POOL_WINDOWS = (2, 4, 8, 16)
N_EXPERT_GROUPS = 4

LANES = 128
BF16_SUBLANES = 16
VMEM_LIMIT_BYTES = 56 * 1024 * 1024

ROW_TILE = 512
ELT_TILE = 256
MOE_TILE = 512
MOE_FF_SPLIT = 4

F32 = jnp.float32
BF16 = jnp.bfloat16


def _params(*sem):
    return pltpu.CompilerParams(dimension_semantics=sem, vmem_limit_bytes=VMEM_LIMIT_BYTES)


def _pick_tile(n, *offsets, choices=(1024, 512, 256, 128)):
    for t in choices:
        if n % t == 0 and all(o % t == 0 for o in offsets):
            return t
    raise ValueError(f"no lane tile divides {n} and offsets {offsets}")


def _dot(a, b):
    return jnp.dot(a, b, preferred_element_type=F32)


def _split_bf16(v, parts):
    out = []
    rem = v
    for _ in range(parts):
        hi = rem.astype(BF16)
        out.append(hi)
        rem = rem - hi.astype(F32)
    return out


def _silu(v):
    return v * jax.nn.sigmoid(v)


_HIGH16 = np.uint32(0xFFFF0000)


def _pack_bf16_pairs(v):
    half = v.shape[1] // 2
    bits = lax.bitcast_convert_type(v, jnp.uint32)
    return (bits[:, :half] >> 16) | (bits[:, half:] & _HIGH16)


def _unpack_bf16_pairs(u):
    lo = lax.bitcast_convert_type(u << 16, F32)
    hi = lax.bitcast_convert_type(u & _HIGH16, F32)
    return lo, hi


def _ada_kernel(c_ref, w_ref, b_ref, o_ref):
    s = _silu(c_ref[...]).astype(BF16)
    o_ref[...] = _dot(s, w_ref[...].astype(BF16)) + b_ref[...]


def _ada(c_rows, w, b, l):
    m, d = c_rows.shape
    n = w.shape[2]
    tn = _pick_tile(n, choices=(512, 256, 128))
    return pl.pallas_call(
        _ada_kernel,
        grid=(n // tn,),
        in_specs=[pl.BlockSpec((m, d), lambda j: (0, 0)),
                  pl.BlockSpec((None, d, tn), lambda j: (l, 0, j)),
                  pl.BlockSpec((None, 1, tn), lambda j: (l, 0, j))],
        out_specs=pl.BlockSpec((m, tn), lambda j: (0, j)),
        out_shape=jax.ShapeDtypeStruct((m, n), F32),
        compiler_params=_params("arbitrary"),
        name="ada_mod",
    )(c_rows, w, b.reshape(b.shape[0], 1, n))


def _prenorm_math(x, w, sc, sh):
    ms = jnp.mean(x * x, axis=-1, keepdims=True)
    y = x * lax.rsqrt(ms + EPS) * w
    return y * (1.0 + sc) + sh


def _prenorm_kernel(x_ref, w_ref, sc_ref, sh_ref, o_ref):
    o_ref[...] = _prenorm_math(x_ref[...], w_ref[...], sc_ref[0], sh_ref[0]).astype(o_ref.dtype)


def _mod_spec(d, col, rows_per_mod, n_mod, tm):
    tiles_per_mod = rows_per_mod // tm
    return pl.BlockSpec((1, 1, d), lambda i: (jnp.minimum(i // tiles_per_mod, n_mod - 1), 0, col))


def _prenorm(xa, w, mods, sc_col, sh_col, seq, n_batch):
    t, d = xa.shape
    tm = ELT_TILE
    return pl.pallas_call(
        _prenorm_kernel,
        grid=(t // tm,),
        in_specs=[pl.BlockSpec((tm, d), lambda i: (i, 0)),
                  pl.BlockSpec((1, d), lambda i: (0, 0)),
                  _mod_spec(d, sc_col, seq, n_batch + 1, tm),
                  _mod_spec(d, sh_col, seq, n_batch + 1, tm)],
        out_specs=pl.BlockSpec((tm, d), lambda i: (i, 0)),
        out_shape=jax.ShapeDtypeStruct((t, d), BF16),
        compiler_params=_params("parallel"),
        name="prenorm",
    )(xa, w.reshape(1, d), mods, mods)


def _final_norm_kernel(x_ref, w_ref, o_ref):
    x = x_ref[...]
    ms = jnp.mean(x * x, axis=-1, keepdims=True)
    o_ref[...] = x * lax.rsqrt(ms + EPS) * w_ref[...]


def _final_norm(xa, w, t_out):
    d = xa.shape[1]
    tm = ELT_TILE
    return pl.pallas_call(
        _final_norm_kernel,
        grid=(t_out // tm,),
        in_specs=[pl.BlockSpec((tm, d), lambda i: (i, 0)),
                  pl.BlockSpec((1, d), lambda i: (0, 0))],
        out_specs=pl.BlockSpec((tm, d), lambda i: (i, 0)),
        out_shape=jax.ShapeDtypeStruct((t_out, d), F32),
        compiler_params=_params("parallel"),
        name="final_norm",
    )(xa, w.reshape(1, d))


def _mm_plain_kernel(a_ref, w_ref, o_ref):
    o_ref[...] = _dot(a_ref[...], w_ref[...]).astype(o_ref.dtype)


def _mm_gate_kernel(a_ref, w_ref, g_ref, o_ref):
    acc = _dot(a_ref[...], w_ref[...])
    o_ref[...] = (jax.nn.sigmoid(g_ref[...].astype(F32)) * acc).astype(o_ref.dtype)


def _mm_gate_acc_kernel(a_ref, w_ref, g_ref, prev_ref, o_ref):
    acc = _dot(a_ref[...], w_ref[...])
    o_ref[...] = (prev_ref[...].astype(F32)
                  + jax.nn.sigmoid(g_ref[...].astype(F32)) * acc).astype(o_ref.dtype)


def _mm_resid_kernel(a_ref, w_ref, x_ref, gm_ref, o_ref):
    acc = _dot(a_ref[...], w_ref[...])
    o_ref[...] = x_ref[...] + gm_ref[0] * acc


def _mm(a, w, l, out_dtype, *, gate=None, gate_off=0, prev=None, resid=None, mods=None,
        mod_col=0, seq=0, n_batch=0, name="mm"):
    t, k = a.shape
    n = w.shape[2]
    tm = ROW_TILE
    tn = _pick_tile(n, gate_off)
    grid = (n // tn, t // tm)
    a_spec = pl.BlockSpec((tm, k), lambda j, i: (i, 0))
    w_spec = pl.BlockSpec((None, k, tn), lambda j, i: (l, 0, j))
    o_spec = pl.BlockSpec((tm, tn), lambda j, i: (i, j))
    goff = gate_off // tn
    g_spec = pl.BlockSpec((tm, tn), lambda j, i: (i, goff + j))
    aliases = {}
    if resid is not None:
        tiles_per_mod = seq // tm
        mcol = mod_col * (n // tn)
        gm_spec = pl.BlockSpec(
            (1, 1, tn), lambda j, i: (jnp.minimum(i // tiles_per_mod, n_batch), 0, mcol + j))
        kern, ins, specs = _mm_resid_kernel, (a, w, resid, mods), [a_spec, w_spec, o_spec, gm_spec]
        aliases = {2: 0}
    elif prev is not None:
        kern, ins, specs = _mm_gate_acc_kernel, (a, w, gate, prev), [a_spec, w_spec, g_spec, o_spec]
        aliases = {3: 0}
    elif gate is not None:
        kern, ins, specs = _mm_gate_kernel, (a, w, gate), [a_spec, w_spec, g_spec]
    else:
        kern, ins, specs = _mm_plain_kernel, (a, w), [a_spec, w_spec]
    return pl.pallas_call(
        kern,
        grid=grid,
        in_specs=specs,
        out_specs=o_spec,
        out_shape=jax.ShapeDtypeStruct((t, n), out_dtype),
        input_output_aliases=aliases,
        compiler_params=_params("parallel", "arbitrary"),
        name=name,
    )(*ins)


def _seqconv_kernel(prev_ref, cur_ref, next_ref, w_ref, b_ref, o_ref, scr, *, ts, taps):
    i = pl.program_id(1)
    n = pl.num_programs(1)
    h = BF16_SUBLANES
    scr[pl.ds(0, h), :] = jnp.where(i > 0, prev_ref[...].astype(F32), 0.0)
    scr[pl.ds(h, ts), :] = cur_ref[...].astype(F32)
    scr[pl.ds(h + ts, h), :] = jnp.where(i < n - 1, next_ref[...].astype(F32), 0.0)
    acc = jnp.broadcast_to(b_ref[...], (ts, b_ref.shape[1]))
    for k in range(taps):
        acc = acc + w_ref[k:k + 1, :] * scr[pl.ds(h - taps // 2 + k, ts), :]
    o_ref[...] = _silu(acc).astype(o_ref.dtype)


def _seqconv(p, w, b, row0, seq, n_seq):
    taps, width = w.shape
    ts = min(seq, 512)
    tc = _pick_tile(width, choices=(512, 256, 128))
    h = BF16_SUBLANES
    tiles = seq // ts
    cur0 = row0 // ts
    halo0 = row0 // h
    per_tile = ts // h
    last_halo = (row0 + n_seq * seq) // h - 1

    def prev_map(s, i, j):
        return (jnp.maximum(halo0 + (s * tiles + i) * per_tile - 1, 0), j)

    def next_map(s, i, j):
        return (jnp.minimum(halo0 + (s * tiles + i + 1) * per_tile, last_halo), j)

    return pl.pallas_call(
        functools.partial(_seqconv_kernel, ts=ts, taps=taps),
        grid=(n_seq, tiles, width // tc),
        in_specs=[pl.BlockSpec((h, tc), prev_map),
                  pl.BlockSpec((ts, tc), lambda s, i, j: (cur0 + s * tiles + i, j)),
                  pl.BlockSpec((h, tc), next_map),
                  pl.BlockSpec((taps, tc), lambda s, i, j: (0, j)),
                  pl.BlockSpec((1, tc), lambda s, i, j: (0, j))],
        out_specs=pl.BlockSpec((ts, tc), lambda s, i, j: (s * tiles + i, j)),
        out_shape=jax.ShapeDtypeStruct((n_seq * seq, width), BF16),
        scratch_shapes=[pltpu.VMEM((ts + 2 * h, tc), F32)],
        compiler_params=_params("parallel", "parallel", "parallel"),
        name="ssd_seqconv",
    )(p, p, p, w, b.reshape(1, width))


def _ssd_kernel(xbc_ref, dt_ref, alog_ref, dtb_ref, dsk_ref, tri_ref, exp_ref, h0_ref,
                y_ref, hfin_ref, st, *, inner, n_groups, n_state, heads_per_group, head_dim):
    c = pl.program_id(2)
    q = dt_ref.shape[0]
    rp = heads_per_group * head_dim
    gn = n_groups * n_state

    @pl.when(c == 0)
    def _():
        st[...] = h0_ref[0, 0]

    x = dt_ref[...] + dtb_ref[0]
    dt = jnp.maximum(x, 0.0) + jnp.log1p(jnp.exp(-jnp.abs(x)))
    a = dt * (-jnp.exp(alog_ref[0]))
    tri = tri_ref[0]
    tri_b = tri.astype(BF16)
    acs = sum(_dot(tri_b, part) for part in _split_bf16(a, 3))
    atot = jnp.sum(a, axis=0, keepdims=True)
    acs_t = acs.T
    stack = jnp.concatenate(
        [dt, jnp.exp(acs), jnp.exp(atot - acs), jnp.broadcast_to(jnp.exp(atot), (8, LANES))], axis=0)
    expand = exp_ref[...]
    wide = sum(_dot(part, expand) for part in _split_bf16(stack, 2))
    dt_x = wide[0:q]
    eacs_x = wide[q:2 * q]
    edec_x = wide[2 * q:3 * q]
    etot_x = wide[3 * q:3 * q + 1]

    xs = xbc_ref[:, 0:inner].astype(F32)
    xdt = xs * dt_x
    xdt_b = xdt.astype(BF16)
    xw_b = (xdt * edec_x).astype(BF16)
    mask = tri > 0.0
    lane_head = lax.broadcasted_iota(jnp.int32, (1, rp), 1) // head_dim
    dsk = dsk_ref[0]

    for g in range(n_groups):
        bg = xbc_ref[:, inner + g * n_state: inner + (g + 1) * n_state]
        cg = xbc_ref[:, inner + gn + g * n_state: inner + gn + (g + 1) * n_state]
        bg_t = bg.astype(F32).T.astype(BF16)
        cb = _dot(cg, bg_t)
        sl = slice(g * rp, (g + 1) * rp)
        chunk_state = _dot(bg_t, xw_b[:, sl])
        st_g = st[g]
        y_off = _dot(cg, st_g.astype(BF16)) * eacs_x[:, sl]
        ms, xm = [], []
        for r in range(heads_per_group):
            hh = g * heads_per_group + r
            seg = acs[:, hh:hh + 1] - acs_t[hh:hh + 1, :]
            decay = jnp.exp(jnp.where(mask, seg, -jnp.inf))
            ms.append((cb * decay).astype(BF16))
            xm.append(jnp.where(lane_head == r, xdt_b[:, sl], jnp.zeros_like(xdt_b[:, sl])))
        y_diag = _dot(jnp.concatenate(ms, axis=1), jnp.concatenate(xm, axis=0))
        y = y_diag + y_off + dsk[:, sl] * xs[:, sl]
        y_ref[0, :, sl] = y.astype(y_ref.dtype)
        st[g] = st_g * etot_x[:, sl] + chunk_state

    @pl.when(c == pl.num_programs(2) - 1)
    def _():
        hfin_ref[0, 0] = st[...]


def _ssd(xbc, dtr, dt_row0, a_log, dt_bias, d_skip, h0, seq, n_seq, geo):
    inner, n_groups, n_state, hpg, head_dim = geo
    heads = n_groups * hpg
    q = SSD_CHUNK
    nc = seq // q
    rp = hpg * head_dim
    xbc_w = xbc.shape[1]

    def pad_heads(v):
        return jnp.pad(v.astype(F32), ((0, 0), (0, LANES - heads))).reshape(2, 1, LANES)

    dsk = jnp.repeat(d_skip.astype(F32), head_dim, axis=-1).reshape(2, 1, inner)
    low = np.tril(np.ones((q, q), np.float32))
    tri = jnp.asarray(np.stack([low, low.T]))
    expand = np.zeros((LANES, inner), np.float32)
    for hh in range(heads):
        expand[hh, hh * head_dim:(hh + 1) * head_dim] = 1.0
    expand = jnp.asarray(expand, dtype=BF16)
    dt_blk0 = dt_row0 // q

    def chunk(s, d, c):
        return s * nc + c + d * (nc - 1 - 2 * c)

    kern = functools.partial(_ssd_kernel, inner=inner, n_groups=n_groups, n_state=n_state,
                             heads_per_group=hpg, head_dim=head_dim)
    y, hfin = pl.pallas_call(
        kern,
        grid=(n_seq, 2, nc),
        in_specs=[pl.BlockSpec((q, xbc_w), lambda s, d, c: (chunk(s, d, c), 0)),
                  pl.BlockSpec((q, LANES), lambda s, d, c: (dt_blk0 + chunk(s, d, c), d)),
                  pl.BlockSpec((1, 1, LANES), lambda s, d, c: (d, 0, 0)),
                  pl.BlockSpec((1, 1, LANES), lambda s, d, c: (d, 0, 0)),
                  pl.BlockSpec((1, 1, inner), lambda s, d, c: (d, 0, 0)),
                  pl.BlockSpec((1, q, q), lambda s, d, c: (d, 0, 0)),
                  pl.BlockSpec((LANES, inner), lambda s, d, c: (0, 0)),
                  pl.BlockSpec((1, 1, n_groups, n_state, rp), lambda s, d, c: (s, d, 0, 0, 0))],
        out_specs=[pl.BlockSpec((1, q, inner), lambda s, d, c: (d, chunk(s, d, c), 0)),
                   pl.BlockSpec((1, 1, n_groups, n_state, rp), lambda s, d, c: (s, d, 0, 0, 0))],
        out_shape=[jax.ShapeDtypeStruct((2, n_seq * seq, inner), BF16),
                   jax.ShapeDtypeStruct((n_seq, 2, n_groups, n_state, rp), F32)],
        scratch_shapes=[pltpu.VMEM((n_groups, n_state, rp), F32)],
        compiler_params=_params("parallel", "parallel", "arbitrary"),
        name="ssd_scan",
    )(xbc, dtr, pad_heads(a_log), pad_heads(dt_bias), dsk, tri, expand, h0)
    return y, hfin


def _two_source_specs(block, n_lat, lead):
    def lat_map(i):
        return lead + (jnp.minimum(i, n_lat - 1), 0)

    def ctx_map(i):
        return lead + (jnp.maximum(i - n_lat, 0), 0)
    return pl.BlockSpec(block, lat_map), pl.BlockSpec(block, ctx_map)


def _gate_norm_kernel(yl_ref, yc_ref, z_ref, w_ref, o_ref, *, n_lat):
    def run(y_ref):
        v = (y_ref[0].astype(F32) + y_ref[1].astype(F32)) * _silu(z_ref[...].astype(F32))
        ms = jnp.mean(v * v, axis=-1, keepdims=True)
        o_ref[...] = (v * lax.rsqrt(ms + EPS) * w_ref[...]).astype(o_ref.dtype)

    is_lat = pl.program_id(0) < n_lat
    pl.when(is_lat)(lambda: run(yl_ref))
    pl.when(jnp.logical_not(is_lat))(lambda: run(yc_ref))


def _gate_norm(y_lat, y_ctx, p, z_off, w):
    inner = y_lat.shape[2]
    tm = ELT_TILE
    n_lat = y_lat.shape[1] // tm
    t = y_lat.shape[1] + y_ctx.shape[1]
    zb = z_off // inner
    yl_spec, yc_spec = _two_source_specs((2, tm, inner), n_lat, (0,))
    return pl.pallas_call(
        functools.partial(_gate_norm_kernel, n_lat=n_lat),
        grid=(t // tm,),
        in_specs=[yl_spec, yc_spec,
                  pl.BlockSpec((tm, inner), lambda i: (i, zb)),
                  pl.BlockSpec((1, inner), lambda i: (0, 0))],
        out_specs=pl.BlockSpec((tm, inner), lambda i: (i, 0)),
        out_shape=jax.ShapeDtypeStruct((t, inner), BF16),
        compiler_params=_params("parallel"),
        name="ssd_gate_norm",
    )(y_lat, y_ctx, p, w.reshape(1, inner))


def _pool_kernel(u_ref, band_ref, invc_ref, wp_ref, ps_ref, g_ref, prev_ref, o_ref):
    u = u_ref[...]
    win_sum = _dot(band_ref[0], u)
    m = win_sum * invc_ref[0][:, 0:1] - u.astype(F32)
    yb = _dot(m.astype(BF16), wp_ref[0]) * ps_ref[...]
    o_ref[...] = (prev_ref[...].astype(F32)
                  + jax.nn.sigmoid(g_ref[...].astype(F32)) * yb).astype(o_ref.dtype)


def _pool_tables(tp, row_len):
    nw = len(POOL_WINDOWS)
    band = np.zeros((nw, tp, tp), np.float32)
    invc = np.zeros((nw, tp, LANES), np.float32)
    j = np.arange(row_len)
    for k, w in enumerate(POOL_WINDOWS):
        lo = np.clip(j - w // 2, 0, row_len - 1)
        hi = np.clip(j + w // 2 - 1, 0, row_len - 1)
        for r0 in range(0, tp, row_len):
            for jj in range(row_len):
                band[k, r0 + jj, r0 + lo[jj]: r0 + hi[jj] + 1] = 1.0
                invc[k, r0 + jj, :] = 1.0 / (hi[jj] - lo[jj] + 1)
    return jnp.asarray(band, dtype=BF16), jnp.asarray(invc)


def _pool(p, merged, w_pool, l, pool_scale, u_off, gate_off, row0, n_rows, row_len):
    _, nw, pg, po = w_pool.shape
    d = merged.shape[1]
    tp = max(row_len, min(512, n_rows))
    band, invc = _pool_tables(tp, row_len)
    ub = u_off // pg
    gb = gate_off // po
    r0 = row0 // tp
    return pl.pallas_call(
        _pool_kernel,
        grid=(n_rows // tp, nw),
        in_specs=[pl.BlockSpec((tp, pg), lambda i, k: (r0 + i, ub + k)),
                  pl.BlockSpec((1, tp, tp), lambda i, k: (k, 0, 0)),
                  pl.BlockSpec((1, tp, LANES), lambda i, k: (k, 0, 0)),
                  pl.BlockSpec((None, 1, pg, po), lambda i, k: (l, k, 0, 0)),
                  pl.BlockSpec((1, po), lambda i, k: (0, k)),
                  pl.BlockSpec((tp, po), lambda i, k: (r0 + i, gb + k)),
                  pl.BlockSpec((tp, po), lambda i, k: (r0 + i, k))],
        out_specs=pl.BlockSpec((tp, po), lambda i, k: (r0 + i, k)),
        out_shape=jax.ShapeDtypeStruct(merged.shape, merged.dtype),
        input_output_aliases={6: 0},
        compiler_params=_params("parallel", "arbitrary"),
        name="pool_branch",
    )(p, band, invc, w_pool, pool_scale.reshape(1, d), p, merged)


def _confconv_kernel(a_ref, b_ref, w_ref, bias_ref, o_ref, scr, *, seq, taps, stride, chunk, pad):
    lanes = a_ref.shape[1]
    first = pad - (taps // 2) * stride
    scr[pl.ds(0, pad), :] = jnp.zeros((pad, lanes), F32)
    scr[pl.ds(pad + seq, pad), :] = jnp.zeros((pad, lanes), F32)
    scr[pl.ds(pad, seq), :] = a_ref[...].astype(F32) * jax.nn.sigmoid(b_ref[...].astype(F32))

    def do_chunk(base):
        acc = jnp.broadcast_to(bias_ref[...], (chunk, lanes))
        for k in range(taps):
            acc = acc + w_ref[k:k + 1, :] * scr[pl.ds(base + first + k * stride, chunk), :]
        o_ref[pl.ds(base, chunk), :] = acc

    n_chunks = seq // chunk
    if n_chunks == 1:
        do_chunk(0)
    else:
        def body(ci, carry):
            do_chunk(pl.multiple_of(ci * chunk, chunk))
            return carry
        lax.fori_loop(0, n_chunks, body, 0)


def _confconv(p, w, b, a_off, row0, seq, n_seq, stride):
    taps, cw = w.shape
    tc = LANES
    chunk = min(seq, 256)
    pad_alloc = -(-((taps // 2) * stride) // 8) * 8
    ab = a_off // tc
    bb = (a_off + cw) // tc
    r0 = row0 // seq
    kern = functools.partial(_confconv_kernel, seq=seq, taps=taps, stride=stride, chunk=chunk,
                             pad=pad_alloc)
    return pl.pallas_call(
        kern,
        grid=(n_seq, cw // tc),
        in_specs=[pl.BlockSpec((seq, tc), lambda s, j: (r0 + s, ab + j)),
                  pl.BlockSpec((seq, tc), lambda s, j: (r0 + s, bb + j)),
                  pl.BlockSpec((taps, tc), lambda s, j: (0, j)),
                  pl.BlockSpec((1, tc), lambda s, j: (0, j))],
        out_specs=pl.BlockSpec((seq, tc), lambda s, j: (s, j)),
        out_shape=jax.ShapeDtypeStruct((n_seq * seq, cw), F32),
        scratch_shapes=[pltpu.VMEM((seq + 2 * pad_alloc, tc), F32)],
        compiler_params=_params("parallel", "parallel"),
        name="conf_conv",
    )(p, p, w, b.reshape(1, cw))


def _ln_silu_kernel(xl_ref, xc_ref, w_ref, b_ref, o_ref, *, n_lat):
    def run(x_ref):
        x = x_ref[...]
        mu = jnp.mean(x, axis=-1, keepdims=True)
        xc = x - mu
        var = jnp.mean(xc * xc, axis=-1, keepdims=True)
        y = xc * lax.rsqrt(var + EPS) * w_ref[...] + b_ref[...]
        o_ref[...] = _silu(y).astype(o_ref.dtype)

    is_lat = pl.program_id(0) < n_lat
    pl.when(is_lat)(lambda: run(xl_ref))
    pl.when(jnp.logical_not(is_lat))(lambda: run(xc_ref))


def _ln_silu(v_lat, v_ctx, w, b):
    cw = v_lat.shape[1]
    tm = ELT_TILE
    n_lat = v_lat.shape[0] // tm
    t = v_lat.shape[0] + v_ctx.shape[0]
    xl_spec, xc_spec = _two_source_specs((tm, cw), n_lat, ())
    return pl.pallas_call(
        functools.partial(_ln_silu_kernel, n_lat=n_lat),
        grid=(t // tm,),
        in_specs=[xl_spec, xc_spec,
                  pl.BlockSpec((1, cw), lambda i: (0, 0)),
                  pl.BlockSpec((1, cw), lambda i: (0, 0))],
        out_specs=pl.BlockSpec((tm, cw), lambda i: (i, 0)),
        out_shape=jax.ShapeDtypeStruct((t, cw), BF16),
        compiler_params=_params("parallel"),
        name="conf_ln_silu",
    )(v_lat, v_ctx, w.reshape(1, cw), b.reshape(1, cw))


def _top1(rows):
    best, idx = rows[0], jnp.zeros(rows[0].shape, jnp.int32)
    for j in range(1, len(rows)):
        better = rows[j] > best
        idx = jnp.where(better, j, idx)
        best = jnp.where(better, rows[j], best)
    return best, idx


def _router_kernel(x_ref, w_ref, sc_ref, sh_ref, wr_ref, rb_ref, h_ref, sel_ref, *, n_experts):
    h2 = _prenorm_math(x_ref[...], w_ref[...], sc_ref[0], sh_ref[0])
    h_hi, h_lo = _split_bf16(h2, 2)
    h_ref[...] = _pack_bf16_pairs(h_hi.astype(F32))
    w_hi, w_lo = _split_bf16(wr_ref[...], 2)
    nt = (((1,), (1,)), ((), ()))

    def dg(a, b):
        return lax.dot_general(a, b, nt, preferred_element_type=F32)

    logits = dg(w_hi, h_hi) + dg(w_hi, h_lo) + dg(w_lo, h_hi)
    scores = jax.nn.sigmoid(logits)
    sel = scores + rb_ref[:, 0:1]
    per_group = n_experts // N_EXPERT_GROUPS
    sel_rows = [sel[e:e + 1, :] for e in range(n_experts)]
    score_rows = [scores[e:e + 1, :] for e in range(n_experts)]
    neg = jnp.full(sel_rows[0].shape, -jnp.inf, F32)

    group_scores = []
    for g in range(N_EXPERT_GROUPS):
        rows = sel_rows[g * per_group:(g + 1) * per_group]
        m1, i1 = _top1(rows)
        m2, _ = _top1([jnp.where(i1 == j, neg, rows[j]) for j in range(per_group)])
        group_scores.append(m1 + m2)
    _, best_group = _top1(group_scores)

    masked = [jnp.where(best_group == e // per_group, sel_rows[e], neg) for e in range(n_experts)]
    _, e1 = _top1(masked)
    _, e2 = _top1([jnp.where(e1 == e, neg, masked[e]) for e in range(n_experts)])
    s1 = sum(jnp.where(e1 == e, score_rows[e], 0.0) for e in range(n_experts))
    s2 = sum(jnp.where(e2 == e, score_rows[e], 0.0) for e in range(n_experts))
    tot = s1 + s2
    sel_ref[...] = jnp.zeros(sel_ref.shape, F32)
    sel_ref[0:1, :] = e1.astype(F32)
    sel_ref[1:2, :] = e2.astype(F32)
    sel_ref[2:3, :] = s1 / tot
    sel_ref[3:4, :] = s2 / tot


def _router(xa, w, mods, sc_col, sh_col, seq, n_batch, w_router, router_bias):
    t, d = xa.shape
    n_experts = w_router.shape[1]
    tm = ELT_TILE
    wr_t = w_router.astype(F32).T
    rb = jnp.broadcast_to(router_bias.astype(F32)[:, None], (n_experts, LANES))
    return pl.pallas_call(
        functools.partial(_router_kernel, n_experts=n_experts),
        grid=(t // tm,),
        in_specs=[pl.BlockSpec((tm, d), lambda i: (i, 0)),
                  pl.BlockSpec((1, d), lambda i: (0, 0)),
                  _mod_spec(d, sc_col, seq, n_batch + 1, tm),
                  _mod_spec(d, sh_col, seq, n_batch + 1, tm),
                  pl.BlockSpec((n_experts, d), lambda i: (0, 0)),
                  pl.BlockSpec((n_experts, LANES), lambda i: (0, 0))],
        out_specs=[pl.BlockSpec((tm, d // 2), lambda i: (i, 0)),
                   pl.BlockSpec((8, tm), lambda i: (0, i))],
        out_shape=[jax.ShapeDtypeStruct((t, d // 2), jnp.uint32),
                   jax.ShapeDtypeStruct((8, t), F32)],
        compiler_params=_params("parallel"),
        name="moe_prenorm_router",
    )(xa, w.reshape(1, d), mods, mods, wr_t, rb)


def _row_copy(src_hbm, row, dst, slot, sem):
    return pltpu.make_async_copy(src_hbm.at[pl.ds(row, 1), :], dst.at[pl.ds(slot, 1), :], sem)


def _gather_rows(src_hbm, idx_ref, dst, sem, n_rows):
    def issue(r, carry):
        _row_copy(src_hbm, idx_ref[0, 0, r], dst, r, sem).start()
        return carry
    lax.fori_loop(0, n_rows, issue, 0)


def _wait_rows(src_hbm, dst, sem, n_rows):
    pltpu.make_async_copy(src_hbm.at[pl.ds(0, n_rows), :], dst, sem).wait()


def _moe_kernel(te_ref, nv_ref, tok_ref, tok_next_ref, h_hbm, wg_ref, wu_ref, wd_ref, o_ref,
                xbuf, xb, acc, sem, *, tm):
    i = pl.program_id(0)
    f = pl.program_id(1)
    n_valid = nv_ref[0]
    valid = i < n_valid
    slot = lax.rem(i, 2)
    half = xb.shape[1] // 2

    @pl.when(f == 0)
    def _():
        @pl.when(i == 0)
        def _():
            _gather_rows(h_hbm, tok_ref, xbuf.at[0], sem.at[0], tm)

        @pl.when(i + 1 < n_valid)
        def _():
            _gather_rows(h_hbm, tok_next_ref, xbuf.at[1 - slot], sem.at[1 - slot], tm)

        @pl.when(valid)
        def _():
            _wait_rows(h_hbm, xbuf.at[slot], sem.at[slot], tm)
            lo, hi = _unpack_bf16_pairs(xbuf[slot])
            xb[:, :half] = lo.astype(BF16)
            xb[:, half:] = hi.astype(BF16)

    @pl.when(valid)
    def _():
        x = xb[...]
        hid = _silu(_dot(x, wg_ref[...])) * _dot(x, wu_ref[...])
        part = _dot(hid.astype(BF16), wd_ref[...])

        @pl.when(f == 0)
        def _():
            acc[...] = part

        @pl.when(f > 0)
        def _():
            acc[...] += part

    @pl.when(f == pl.num_programs(1) - 1)
    def _():
        @pl.when(valid)
        def _():
            o_ref[...] = _pack_bf16_pairs(acc[...].astype(BF16).astype(F32))

        @pl.when(jnp.logical_not(valid))
        def _():
            o_ref[...] = jnp.zeros(o_ref.shape, o_ref.dtype)


def _moe(h2p, tile_expert, n_valid, row_token, wg, wu, wd, l):
    half = h2p.shape[1]
    d = 2 * half
    ffh = wg.shape[4]
    tm = MOE_TILE
    n_tiles = tile_expert.shape[0]
    tok = row_token.reshape(n_tiles, 1, tm)
    w_in_spec = pl.BlockSpec((None, None, None, d, ffh), lambda i, f, te, nv: (l, te[i], f, 0, 0))
    grid_spec = pltpu.PrefetchScalarGridSpec(
        num_scalar_prefetch=2,
        grid=(n_tiles, MOE_FF_SPLIT),
        in_specs=[pl.BlockSpec((1, 1, tm), lambda i, f, te, nv: (i, 0, 0), memory_space=pltpu.SMEM),
                  pl.BlockSpec((1, 1, tm), lambda i, f, te, nv: (jnp.minimum(i + 1, n_tiles - 1), 0, 0),
                               memory_space=pltpu.SMEM),
                  pl.BlockSpec(memory_space=pl.ANY),
                  w_in_spec, w_in_spec,
                  pl.BlockSpec((None, None, None, ffh, d), lambda i, f, te, nv: (l, te[i], f, 0, 0))],
        out_specs=pl.BlockSpec((tm, half), lambda i, f, te, nv: (i, 0)),
        scratch_shapes=[pltpu.VMEM((2, tm, half), jnp.uint32), pltpu.VMEM((tm, d), BF16),
                        pltpu.VMEM((tm, d), F32), pltpu.SemaphoreType.DMA((2,))],
    )
    return pl.pallas_call(
        functools.partial(_moe_kernel, tm=tm),
        grid_spec=grid_spec,
        out_shape=jax.ShapeDtypeStruct((n_tiles * tm, half), jnp.uint32),
        compiler_params=_params("arbitrary", "arbitrary"),
        name="moe_experts",
    )(tile_expert, n_valid, tok, tok, h2p, wg, wu, wd)


def _combine_kernel(p0_ref, p1_ref, p0n_ref, p1n_ref, x_ref, gm_ref, gw_ref, ys_hbm, o_ref,
                    buf, sem, *, tc):
    i = pl.program_id(0)
    slot = lax.rem(i, 2)
    half = buf.shape[3]

    def gather(pa_ref, pb_ref, s):
        _gather_rows(ys_hbm, pa_ref, buf.at[s, 0], sem.at[s], tc)
        _gather_rows(ys_hbm, pb_ref, buf.at[s, 1], sem.at[s], tc)

    @pl.when(i == 0)
    def _():
        gather(p0_ref, p1_ref, 0)

    @pl.when(i + 1 < pl.num_programs(0))
    def _():
        gather(p0n_ref, p1n_ref, 1 - slot)

    _wait_rows(ys_hbm, buf.at[slot, 0], sem.at[slot], tc)
    _wait_rows(ys_hbm, buf.at[slot, 1], sem.at[slot], tc)
    lo0, hi0 = _unpack_bf16_pairs(buf[slot, 0])
    lo1, hi1 = _unpack_bf16_pairs(buf[slot, 1])
    w0 = gw_ref[:, 0:1]
    w1 = gw_ref[:, LANES:LANES + 1]
    g = gm_ref[0]
    o_ref[:, :half] = x_ref[:, :half] + g[:, :half] * (w0 * lo0 + w1 * lo1)
    o_ref[:, half:] = x_ref[:, half:] + g[:, half:] * (w0 * hi0 + w1 * hi1)


def _combine(xa, ys, pos0, pos1, gate_cols, mods, mod_col, seq, n_batch):
    t, d = xa.shape
    half = d // 2
    tc = ELT_TILE
    n = t // tc
    p0 = pos0.reshape(n, 1, tc)
    p1 = pos1.reshape(n, 1, tc)
    cur = functools.partial(pl.BlockSpec, (1, 1, tc), lambda i: (i, 0, 0), memory_space=pltpu.SMEM)
    nxt = functools.partial(pl.BlockSpec, (1, 1, tc), lambda i: (jnp.minimum(i + 1, n - 1), 0, 0),
                            memory_space=pltpu.SMEM)
    return pl.pallas_call(
        functools.partial(_combine_kernel, tc=tc),
        grid=(n,),
        in_specs=[cur(), cur(), nxt(), nxt(),
                  pl.BlockSpec((tc, d), lambda i: (i, 0)),
                  _mod_spec(d, mod_col, seq, n_batch + 1, tc),
                  pl.BlockSpec((tc, 2 * LANES), lambda i: (i, 0)),
                  pl.BlockSpec(memory_space=pl.ANY)],
        out_specs=pl.BlockSpec((tc, d), lambda i: (i, 0)),
        out_shape=jax.ShapeDtypeStruct((t, d), F32),
        scratch_shapes=[pltpu.VMEM((2, 2, tc, half), jnp.uint32), pltpu.SemaphoreType.DMA((2,))],
        input_output_aliases={4: 0},
        compiler_params=_params("arbitrary"),
        name="moe_combine",
    )(p0, p1, p0, p1, xa, mods, gate_cols, ys)


def _routing_tables(sel, n_experts, tm):
    t = sel.shape[1]
    e = sel[0:2].astype(jnp.int32).reshape(-1)
    onehot = (e[:, None] == jnp.arange(n_experts, dtype=jnp.int32)[None, :]).astype(jnp.int32)
    incl = jnp.cumsum(onehot, axis=0)
    rank = jnp.sum((incl - onehot) * onehot, axis=1)
    counts = incl[-1]
    tiles = (counts + tm - 1) // tm
    tile_end = jnp.cumsum(tiles)
    tile_start = tile_end - tiles
    pos = tile_start[e] * tm + rank
    n_tiles = (2 * t) // tm + n_experts
    token = jnp.tile(jnp.arange(t, dtype=jnp.int32), 2)
    row_token = jnp.zeros((n_tiles * tm,), jnp.int32).at[pos].set(token)
    n_valid = tile_end[-1]
    tile_ids = jnp.minimum(jnp.arange(n_tiles, dtype=jnp.int32), n_valid - 1)
    tile_expert = jnp.sum((tile_ids[:, None] >= tile_end[None, :]).astype(jnp.int32), axis=1)
    gate_cols = jnp.broadcast_to(sel[2:4].T[:, :, None], (t, 2, LANES)).reshape(t, 2 * LANES)
    return tile_expert, n_valid.reshape(1), row_token, pos[:t], pos[t:], gate_cols


def kernel(x, c, ctx, c_ctx, w_ada, b_ada, norm1_w, norm2_w, w_in, ssd_conv_w, ssd_conv_b,
           ssd_A_log, ssd_dt_bias, ssd_D, ssd_norm_w, w_ssd_out, w_pool, pool_scale,
           conf_conv_w, conf_conv_b, conf_ln_w, conf_ln_b, w_conf_out, w_out, w_router,
           router_bias, w_exp_gate, w_exp_up, w_exp_down, final_norm_w):
    n_batch, seq, d = x.shape
    ctx_len = ctx.shape[1]
    depth = w_in.shape[0]
    inner = w_ssd_out.shape[1]
    heads = ssd_A_log.shape[-1]
    xbc_w = ssd_conv_w.shape[-1]
    gn = (xbc_w - inner) // 2
    n_groups = gn // SSD_STATE
    hpg = heads // n_groups
    geo = (inner, n_groups, SSD_STATE, hpg, SSD_HEAD_DIM)
    pool_w = w_pool.shape[1] * w_pool.shape[2]
    conf_w = conf_conv_w.shape[-1]
    n_experts = w_router.shape[1]
    assert heads <= LANES and hpg * SSD_HEAD_DIM * n_groups == inner
    assert seq % GRID_W == 0 and seq % ROW_TILE == 0 and ctx_len % ELT_TILE == 0

    off_dt = xbc_w
    off_z = off_dt + 2 * heads
    off_pool = off_z + inner
    off_conf = off_pool + pool_w
    off_gate = off_conf + 2 * conf_w
    p_z = xbc_w
    p_pool = p_z + inner
    p_conf = p_pool + pool_w
    p_gate = p_conf + 2 * conf_w

    t_lat = n_batch * seq
    t_ctx = n_batch * ctx_len
    xa = jnp.concatenate([x.reshape(t_lat, d), ctx.reshape(t_ctx, d)], axis=0)
    t_all = t_lat + t_ctx
    assert t_all % ROW_TILE == 0 and ctx_len * n_batch <= seq

    c_rows = jnp.zeros((BF16_SUBLANES, d), F32).at[:n_batch].set(c).at[n_batch].set(c_ctx)
    h0 = jnp.zeros((n_batch, 2, n_groups, SSD_STATE, hpg * SSD_HEAD_DIM), F32)

    w_main = jnp.concatenate([w_in[:, :, :off_dt], w_in[:, :, off_z:]], axis=2).astype(BF16)
    w_dt = jnp.zeros((depth, d, 2 * LANES), F32)
    w_dt = w_dt.at[:, :, :heads].set(w_in[:, :, off_dt:off_dt + heads])
    w_dt = w_dt.at[:, :, LANES:LANES + heads].set(w_in[:, :, off_dt + heads:off_z]).astype(BF16)
    w_ssd_out_b = w_ssd_out.astype(BF16)
    w_pool_b = w_pool.astype(BF16)
    w_conf_out_b = w_conf_out.astype(BF16)
    w_out_b = w_out.astype(BF16)
    ff = w_exp_gate.shape[-1]
    ffh = ff // MOE_FF_SPLIT

    def split_cols(w):
        return w.astype(BF16).reshape(depth, n_experts, d, MOE_FF_SPLIT, ffh).transpose(0, 1, 3, 2, 4)

    wg_b = split_cols(w_exp_gate)
    wu_b = split_cols(w_exp_up)
    wd_b = w_exp_down.astype(BF16).reshape(depth, n_experts, MOE_FF_SPLIT, ffh, d)

    for l in range(depth):
        mods = _ada(c_rows, w_ada, b_ada, l).reshape(BF16_SUBLANES, 1, 6 * d)

        h = _prenorm(xa, norm1_w[l], mods, 1, 0, seq, n_batch)
        p = _mm(h, w_main, l, BF16, name="in_proj")
        dtr = _mm(h, w_dt, l, F32, name="dt_proj")

        xbc_ctx = _seqconv(p, ssd_conv_w[l], ssd_conv_b[l], t_lat, ctx_len, n_batch)
        xbc_lat = _seqconv(p, ssd_conv_w[l], ssd_conv_b[l], 0, seq, n_batch)
        y_ctx, s_ctx = _ssd(xbc_ctx, dtr, t_lat, ssd_A_log[l], ssd_dt_bias[l], ssd_D[l], h0,
                            ctx_len, n_batch, geo)
        y_lat, _ = _ssd(xbc_lat, dtr, 0, ssd_A_log[l], ssd_dt_bias[l], ssd_D[l], s_ctx,
                        seq, n_batch, geo)
        ya = _gate_norm(y_lat, y_ctx, p, p_z, ssd_norm_w[l])
        merged = _mm(ya, w_ssd_out_b, l, BF16, gate=p, gate_off=p_gate, name="ssd_out")

        merged = _pool(p, merged, w_pool_b, l, pool_scale[l], p_pool, p_gate + d, 0, t_lat, GRID_W)
        merged = _pool(p, merged, w_pool_b, l, pool_scale[l], p_pool, p_gate + d, t_lat, t_ctx,
                       ctx_len)

        v_lat = _confconv(p, conf_conv_w[l], conf_conv_b[l], p_conf, 0, seq, n_batch, GRID_W)
        v_ctx = _confconv(p, conf_conv_w[l], conf_conv_b[l], p_conf, t_lat, ctx_len, n_batch, 1)
        ca = _ln_silu(v_lat, v_ctx, conf_ln_w[l], conf_ln_b[l])
        merged = _mm(ca, w_conf_out_b, l, BF16, gate=p, gate_off=p_gate + 2 * d, prev=merged,
                     name="conf_out")

        xa = _mm(merged, w_out_b, l, F32, resid=xa, mods=mods, mod_col=2, seq=seq,
                 n_batch=n_batch, name="mixer_out")

        h2p, sel = _router(xa, norm2_w[l], mods, 4, 3, seq, n_batch, w_router, router_bias)
        tile_expert, n_valid, row_token, pos0, pos1, gate_cols = _routing_tables(
            sel, n_experts, MOE_TILE)
        ys = _moe(h2p, tile_expert, n_valid, row_token, wg_b, wu_b, wd_b, l)
        xa = _combine(xa, ys, pos0, pos1, gate_cols, mods, 5, seq, n_batch)

    return _final_norm(xa, final_norm_w, t_lat).reshape(n_batch, seq, d)
```

```python
import functools

import numpy as np
import jax
import jax.numpy as jnp
from jax import lax
from jax.experimental import pallas as pl
from jax.experimental.pallas import tpu as pltpu

GRID_W = 64
EPS = 1e-6
SSD_HEAD_DIM = 64
SSD_STATE = 128
SSD_CHUNK = 128
POOL_WINDOWS = (2, 4, 8, 16)
N_EXPERT_GROUPS = 4

LANES = 128
BF16_SUBLANES = 16
VMEM_LIMIT_BYTES = 56 * 1024 * 1024

ROW_TILE = 512
ELT_TILE = 256
MOE_TILE = 512
MOE_K_SPLIT = 4
GATHER_UNROLL = 8

F32 = jnp.float32
BF16 = jnp.bfloat16


def _params(*sem):
    return pltpu.CompilerParams(dimension_semantics=sem, vmem_limit_bytes=VMEM_LIMIT_BYTES)


def _pick_tile(n, *offsets, choices=(1024, 512, 256, 128)):
    for t in choices:
        if n % t == 0 and all(o % t == 0 for o in offsets):
            return t
    raise ValueError(f"no lane tile divides {n} and offsets {offsets}")


def _dot(a, b):
    return jnp.dot(a, b, preferred_element_type=F32)


def _split_bf16(v, parts):
    out = []
    rem = v
    for _ in range(parts):
        hi = rem.astype(BF16)
        out.append(hi)
        rem = rem - hi.astype(F32)
    return out


def _silu(v):
    return v * jax.nn.sigmoid(v)


_HIGH16 = np.uint32(0xFFFF0000)


def _pack_halves(lo, hi):
    return ((lax.bitcast_convert_type(lo, jnp.uint32) >> 16)
            | (lax.bitcast_convert_type(hi, jnp.uint32) & _HIGH16))


def _pack_bf16_pairs(v):
    half = v.shape[1] // 2
    return _pack_halves(v[:, :half], v[:, half:])


def _unpack_bf16_pairs(u):
    lo = lax.bitcast_convert_type(u << 16, F32)
    hi = lax.bitcast_convert_type(u & _HIGH16, F32)
    return lo, hi


def _ada_kernel(c_ref, w_ref, b_ref, o_ref):
    s = _silu(c_ref[...]).astype(BF16)
    o_ref[...] = _dot(s, w_ref[...].astype(BF16)) + b_ref[...]


def _ada(c_rows, w, b, l):
    m, d = c_rows.shape
    n = w.shape[2]
    tn = _pick_tile(n, choices=(512, 256, 128))
    return pl.pallas_call(
        _ada_kernel,
        grid=(n // tn,),
        in_specs=[pl.BlockSpec((m, d), lambda j: (0, 0)),
                  pl.BlockSpec((None, d, tn), lambda j: (l, 0, j)),
                  pl.BlockSpec((None, 1, tn), lambda j: (l, 0, j))],
        out_specs=pl.BlockSpec((m, tn), lambda j: (0, j)),
        out_shape=jax.ShapeDtypeStruct((m, n), F32),
        compiler_params=_params("arbitrary"),
        name="ada_mod",
    )(c_rows, w, b.reshape(b.shape[0], 1, n))


def _prenorm_math(x, w, sc, sh):
    ms = jnp.mean(x * x, axis=-1, keepdims=True)
    y = x * lax.rsqrt(ms + EPS) * w
    return y * (1.0 + sc) + sh


def _two_source_specs(block, n_lat, lead):
    def lat_map(i):
        return lead + (jnp.minimum(i, n_lat - 1), 0)

    def ctx_map(i):
        return lead + (jnp.maximum(i - n_lat, 0), 0)
    return pl.BlockSpec(block, lat_map), pl.BlockSpec(block, ctx_map)


def _prenorm_kernel(xl_ref, xc_ref, w_ref, sc_ref, sh_ref, o_ref, *, n_lat):
    def run(x_ref):
        o_ref[...] = _prenorm_math(x_ref[...], w_ref[...], sc_ref[0], sh_ref[0]).astype(o_ref.dtype)

    is_lat = pl.program_id(0) < n_lat
    pl.when(is_lat)(lambda: run(xl_ref))
    pl.when(jnp.logical_not(is_lat))(lambda: run(xc_ref))


def _mod_spec(d, col, rows_per_mod, n_mod, tm):
    tiles_per_mod = rows_per_mod // tm
    return pl.BlockSpec((1, 1, d), lambda i: (jnp.minimum(i // tiles_per_mod, n_mod - 1), 0, col))


def _prenorm(x_lat, x_ctx, w, mods, sc_col, sh_col, seq, n_batch):
    d = x_lat.shape[1]
    tm = ELT_TILE
    n_lat = x_lat.shape[0] // tm
    t = x_lat.shape[0] + x_ctx.shape[0]
    xl_spec, xc_spec = _two_source_specs((tm, d), n_lat, ())
    return pl.pallas_call(
        functools.partial(_prenorm_kernel, n_lat=n_lat),
        grid=(t // tm,),
        in_specs=[xl_spec, xc_spec,
                  pl.BlockSpec((1, d), lambda i: (0, 0)),
                  _mod_spec(d, sc_col, seq, n_batch + 1, tm),
                  _mod_spec(d, sh_col, seq, n_batch + 1, tm)],
        out_specs=pl.BlockSpec((tm, d), lambda i: (i, 0)),
        out_shape=jax.ShapeDtypeStruct((t, d), BF16),
        compiler_params=_params("parallel"),
        name="prenorm",
    )(x_lat, x_ctx, w.reshape(1, d), mods, mods)


def _mm_plain_kernel(a_ref, w_ref, o_ref):
    o_ref[...] = _dot(a_ref[...], w_ref[...]).astype(o_ref.dtype)


def _mm_gate_kernel(a_ref, w_ref, g_ref, o_ref):
    acc = _dot(a_ref[...], w_ref[...])
    o_ref[...] = (jax.nn.sigmoid(g_ref[...].astype(F32)) * acc).astype(o_ref.dtype)


def _mm_gate_acc_kernel(a_ref, w_ref, g_ref, prev_ref, o_ref):
    acc = _dot(a_ref[...], w_ref[...])
    o_ref[...] = (prev_ref[...].astype(F32)
                  + jax.nn.sigmoid(g_ref[...].astype(F32)) * acc).astype(o_ref.dtype)


def _mm_resid_kernel(a_ref, w_ref, x_ref, gm_ref, o_ref):
    acc = _dot(a_ref[...], w_ref[...])
    o_ref[...] = x_ref[...] + gm_ref[0] * acc


def _mm_resid2_kernel(a_ref, w_ref, xl_ref, xc_ref, gm_ref, o_ref, *, n_lat):
    acc = _dot(a_ref[...], w_ref[...])
    is_lat = pl.program_id(1) < n_lat

    @pl.when(is_lat)
    def _():
        o_ref[...] = xl_ref[...] + gm_ref[0] * acc

    @pl.when(jnp.logical_not(is_lat))
    def _():
        o_ref[...] = xc_ref[...] + gm_ref[0] * acc


def _mm(a, w, l, out_dtype, *, gate=None, gate_off=0, prev=None, resid=None, mods=None,
        mod_col=0, seq=0, n_batch=0, name="mm"):
    t, k = a.shape
    n = w.shape[2]
    tm = ROW_TILE
    tn = _pick_tile(n, gate_off)
    grid = (n // tn, t // tm)
    a_spec = pl.BlockSpec((tm, k), lambda j, i: (i, 0))
    w_spec = pl.BlockSpec((None, k, tn), lambda j, i: (l, 0, j))
    o_spec = pl.BlockSpec((tm, tn), lambda j, i: (i, j))
    goff = gate_off // tn
    g_spec = pl.BlockSpec((tm, tn), lambda j, i: (i, goff + j))
    aliases = {}
    if resid is not None:
        tiles_per_mod = seq // tm
        mcol = mod_col * (n // tn)
        gm_spec = pl.BlockSpec(
            (1, 1, tn), lambda j, i: (jnp.minimum(i // tiles_per_mod, n_batch), 0, mcol + j))
        if isinstance(resid, tuple):
            x_lat, x_ctx = resid
            n_lat = x_lat.shape[0] // tm
            xl_spec = pl.BlockSpec((tm, tn), lambda j, i: (jnp.minimum(i, n_lat - 1), j))
            xc_spec = pl.BlockSpec((tm, tn), lambda j, i: (jnp.maximum(i - n_lat, 0), j))
            kern = functools.partial(_mm_resid2_kernel, n_lat=n_lat)
            ins, specs = (a, w, x_lat, x_ctx, mods), [a_spec, w_spec, xl_spec, xc_spec, gm_spec]
        else:
            kern, ins, specs = _mm_resid_kernel, (a, w, resid, mods), [a_spec, w_spec, o_spec, gm_spec]
            aliases = {2: 0}
    elif prev is not None:
        kern, ins, specs = _mm_gate_acc_kernel, (a, w, gate, prev), [a_spec, w_spec, g_spec, o_spec]
        aliases = {3: 0}
    elif gate is not None:
        kern, ins, specs = _mm_gate_kernel, (a, w, gate), [a_spec, w_spec, g_spec]
    else:
        kern, ins, specs = _mm_plain_kernel, (a, w), [a_spec, w_spec]
    return pl.pallas_call(
        kern,
        grid=grid,
        in_specs=specs,
        out_specs=o_spec,
        out_shape=jax.ShapeDtypeStruct((t, n), out_dtype),
        input_output_aliases=aliases,
        compiler_params=_params("parallel", "arbitrary"),
        name=name,
    )(*ins)


def _seqconv_kernel(prev_ref, cur_ref, next_ref, w_ref, b_ref, o_ref, scr, *, ts, taps):
    i = pl.program_id(1)
    n = pl.num_programs(1)
    h = BF16_SUBLANES
    scr[pl.ds(0, h), :] = jnp.where(i > 0, prev_ref[...].astype(F32), 0.0)
    scr[pl.ds(h, ts), :] = cur_ref[...].astype(F32)
    scr[pl.ds(h + ts, h), :] = jnp.where(i < n - 1, next_ref[...].astype(F32), 0.0)
    acc = jnp.broadcast_to(b_ref[...], (ts, b_ref.shape[1]))
    for k in range(taps):
        acc = acc + w_ref[k:k + 1, :] * scr[pl.ds(h - taps // 2 + k, ts), :]
    o_ref[...] = _silu(acc).astype(o_ref.dtype)


def _seqconv(p, w, b, row0, seq, n_seq):
    taps, width = w.shape
    ts = min(seq, 512)
    tc = _pick_tile(width, choices=(512, 256, 128))
    h = BF16_SUBLANES
    tiles = seq // ts
    cur0 = row0 // ts
    halo0 = row0 // h
    per_tile = ts // h
    last_halo = (row0 + n_seq * seq) // h - 1

    def prev_map(s, i, j):
        return (jnp.maximum(halo0 + (s * tiles + i) * per_tile - 1, 0), j)

    def next_map(s, i, j):
        return (jnp.minimum(halo0 + (s * tiles + i + 1) * per_tile, last_halo), j)

    return pl.pallas_call(
        functools.partial(_seqconv_kernel, ts=ts, taps=taps),
        grid=(n_seq, tiles, width // tc),
        in_specs=[pl.BlockSpec((h, tc), prev_map),
                  pl.BlockSpec((ts, tc), lambda s, i, j: (cur0 + s * tiles + i, j)),
                  pl.BlockSpec((h, tc), next_map),
                  pl.BlockSpec((taps, tc), lambda s, i, j: (0, j)),
                  pl.BlockSpec((1, tc), lambda s, i, j: (0, j))],
        out_specs=pl.BlockSpec((ts, tc), lambda s, i, j: (s * tiles + i, j)),
        out_shape=jax.ShapeDtypeStruct((n_seq * seq, width), BF16),
        scratch_shapes=[pltpu.VMEM((ts + 2 * h, tc), F32)],
        compiler_params=_params("parallel", "parallel", "parallel"),
        name="ssd_seqconv",
    )(p, p, p, w, b.reshape(1, width))


def _ssd_kernel(xbc_ref, dt_ref, alog_ref, dtb_ref, dsk_ref, tri_ref, exp_ref, h0_ref,
                y_ref, hfin_ref, st, *, inner, n_groups, n_state, heads_per_group, head_dim):
    c = pl.program_id(2)
    q = dt_ref.shape[0]
    rp = heads_per_group * head_dim
    gn = n_groups * n_state

    @pl.when(c == 0)
    def _():
        st[...] = h0_ref[0, 0]

    x = dt_ref[...] + dtb_ref[0]
    dt = jnp.maximum(x, 0.0) + jnp.log1p(jnp.exp(-jnp.abs(x)))
    a = dt * (-jnp.exp(alog_ref[0]))
    tri = tri_ref[0]
    tri_b = tri.astype(BF16)
    acs = sum(_dot(tri_b, part) for part in _split_bf16(a, 3))
    atot = jnp.sum(a, axis=0, keepdims=True)
    acs_t = acs.T
    stack = jnp.concatenate(
        [dt, jnp.exp(acs), jnp.exp(atot - acs), jnp.broadcast_to(jnp.exp(atot), (8, LANES))], axis=0)
    expand = exp_ref[...]
    wide = sum(_dot(part, expand) for part in _split_bf16(stack, 2))
    dt_x = wide[0:q]
    eacs_x = wide[q:2 * q]
    edec_x = wide[2 * q:3 * q]
    etot_x = wide[3 * q:3 * q + 1]

    xs = xbc_ref[:, 0:inner].astype(F32)
    xdt = xs * dt_x
    xdt_b = xdt.astype(BF16)
    xw_b = (xdt * edec_x).astype(BF16)
    mask = tri > 0.0
    lane_head = lax.broadcasted_iota(jnp.int32, (1, rp), 1) // head_dim
    dsk = dsk_ref[0]

    for g in range(n_groups):
        bg = xbc_ref[:, inner + g * n_state: inner + (g + 1) * n_state]
        cg = xbc_ref[:, inner + gn + g * n_state: inner + gn + (g + 1) * n_state]
        bg_t = bg.astype(F32).T.astype(BF16)
        cb = _dot(cg, bg_t)
        sl = slice(g * rp, (g + 1) * rp)
        chunk_state = _dot(bg_t, xw_b[:, sl])
        st_g = st[g]
        y_off = _dot(cg, st_g.astype(BF16)) * eacs_x[:, sl]
        ms, xm = [], []
        for r in range(heads_per_group):
            hh = g * heads_per_group + r
            seg = acs[:, hh:hh + 1] - acs_t[hh:hh + 1, :]
            decay = jnp.exp(jnp.where(mask, seg, -jnp.inf))
            ms.append((cb * decay).astype(BF16))
            xm.append(jnp.where(lane_head == r, xdt_b[:, sl], jnp.zeros_like(xdt_b[:, sl])))
        y_diag = _dot(jnp.concatenate(ms, axis=1), jnp.concatenate(xm, axis=0))
        y = y_diag + y_off + dsk[:, sl] * xs[:, sl]
        y_ref[0, :, sl] = y.astype(y_ref.dtype)
        st[g] = st_g * etot_x[:, sl] + chunk_state

    @pl.when(c == pl.num_programs(2) - 1)
    def _():
        hfin_ref[0, 0] = st[...]


def _ssd(xbc, dtr, dt_row0, a_log, dt_bias, d_skip, h0, seq, n_seq, geo):
    inner, n_groups, n_state, hpg, head_dim = geo
    heads = n_groups * hpg
    q = SSD_CHUNK
    nc = seq // q
    rp = hpg * head_dim
    xbc_w = xbc.shape[1]

    def pad_heads(v):
        return jnp.pad(v.astype(F32), ((0, 0), (0, LANES - heads))).reshape(2, 1, LANES)

    dsk = jnp.repeat(d_skip.astype(F32), head_dim, axis=-1).reshape(2, 1, inner)
    low = np.tril(np.ones((q, q), np.float32))
    tri = jnp.asarray(np.stack([low, low.T]))
    expand = np.zeros((LANES, inner), np.float32)
    for hh in range(heads):
        expand[hh, hh * head_dim:(hh + 1) * head_dim] = 1.0
    expand = jnp.asarray(expand, dtype=BF16)
    dt_blk0 = dt_row0 // q

    def chunk(s, d, c):
        return s * nc + c + d * (nc - 1 - 2 * c)

    kern = functools.partial(_ssd_kernel, inner=inner, n_groups=n_groups, n_state=n_state,
                             heads_per_group=hpg, head_dim=head_dim)
    y, hfin = pl.pallas_call(
        kern,
        grid=(n_seq, 2, nc),
        in_specs=[pl.BlockSpec((q, xbc_w), lambda s, d, c: (chunk(s, d, c), 0)),
                  pl.BlockSpec((q, LANES), lambda s, d, c: (dt_blk0 + chunk(s, d, c), d)),
                  pl.BlockSpec((1, 1, LANES), lambda s, d, c: (d, 0, 0)),
                  pl.BlockSpec((1, 1, LANES), lambda s, d, c: (d, 0, 0)),
                  pl.BlockSpec((1, 1, inner), lambda s, d, c: (d, 0, 0)),
                  pl.BlockSpec((1, q, q), lambda s, d, c: (d, 0, 0)),
                  pl.BlockSpec((LANES, inner), lambda s, d, c: (0, 0)),
                  pl.BlockSpec((1, 1, n_groups, n_state, rp), lambda s, d, c: (s, d, 0, 0, 0))],
        out_specs=[pl.BlockSpec((1, q, inner), lambda s, d, c: (d, chunk(s, d, c), 0)),
                   pl.BlockSpec((1, 1, n_groups, n_state, rp), lambda s, d, c: (s, d, 0, 0, 0))],
        out_shape=[jax.ShapeDtypeStruct((2, n_seq * seq, inner), BF16),
                   jax.ShapeDtypeStruct((n_seq, 2, n_groups, n_state, rp), F32)],
        scratch_shapes=[pltpu.VMEM((n_groups, n_state, rp), F32)],
        compiler_params=_params("parallel", "parallel", "arbitrary"),
        name="ssd_scan",
    )(xbc, dtr, pad_heads(a_log), pad_heads(dt_bias), dsk, tri, expand, h0)
    return y, hfin


def _gate_norm_kernel(yl_ref, yc_ref, z_ref, w_ref, o_ref, *, n_lat):
    def run(y_ref):
        v = (y_ref[0].astype(F32) + y_ref[1].astype(F32)) * _silu(z_ref[...].astype(F32))
        ms = jnp.mean(v * v, axis=-1, keepdims=True)
        o_ref[...] = (v * lax.rsqrt(ms + EPS) * w_ref[...]).astype(o_ref.dtype)

    is_lat = pl.program_id(0) < n_lat
    pl.when(is_lat)(lambda: run(yl_ref))
    pl.when(jnp.logical_not(is_lat))(lambda: run(yc_ref))


def _gate_norm(y_lat, y_ctx, p, z_off, w):
    inner = y_lat.shape[2]
    tm = ELT_TILE
    n_lat = y_lat.shape[1] // tm
    t = y_lat.shape[1] + y_ctx.shape[1]
    zb = z_off // inner
    yl_spec, yc_spec = _two_source_specs((2, tm, inner), n_lat, (0,))
    return pl.pallas_call(
        functools.partial(_gate_norm_kernel, n_lat=n_lat),
        grid=(t // tm,),
        in_specs=[yl_spec, yc_spec,
                  pl.BlockSpec((tm, inner), lambda i: (i, zb)),
                  pl.BlockSpec((1, inner), lambda i: (0, 0))],
        out_specs=pl.BlockSpec((tm, inner), lambda i: (i, 0)),
        out_shape=jax.ShapeDtypeStruct((t, inner), BF16),
        compiler_params=_params("parallel"),
        name="ssd_gate_norm",
    )(y_lat, y_ctx, p, w.reshape(1, inner))


def _pool_kernel(u_ref, band_ref, invc_ref, wp_ref, ps_ref, g_ref, prev_ref, o_ref):
    u = u_ref[...]
    win_sum = _dot(band_ref[0], u)
    m = win_sum * invc_ref[0][:, 0:1] - u.astype(F32)
    yb = _dot(m.astype(BF16), wp_ref[0]) * ps_ref[...]
    o_ref[...] = (prev_ref[...].astype(F32)
                  + jax.nn.sigmoid(g_ref[...].astype(F32)) * yb).astype(o_ref.dtype)


def _pool_tables(tp, row_len):
    nw = len(POOL_WINDOWS)
    band = np.zeros((nw, tp, tp), np.float32)
    invc = np.zeros((nw, tp, LANES), np.float32)
    j = np.arange(row_len)
    for k, w in enumerate(POOL_WINDOWS):
        lo = np.clip(j - w // 2, 0, row_len - 1)
        hi = np.clip(j + w // 2 - 1, 0, row_len - 1)
        for r0 in range(0, tp, row_len):
            for jj in range(row_len):
                band[k, r0 + jj, r0 + lo[jj]: r0 + hi[jj] + 1] = 1.0
                invc[k, r0 + jj, :] = 1.0 / (hi[jj] - lo[jj] + 1)
    return jnp.asarray(band, dtype=BF16), jnp.asarray(invc)


def _pool(p, merged, w_pool, l, pool_scale, u_off, gate_off, row0, n_rows, row_len):
    _, nw, pg, po = w_pool.shape
    d = merged.shape[1]
    tp = max(row_len, min(512, n_rows))
    band, invc = _pool_tables(tp, row_len)
    ub = u_off // pg
    gb = gate_off // po
    r0 = row0 // tp
    return pl.pallas_call(
        _pool_kernel,
        grid=(n_rows // tp, nw),
        in_specs=[pl.BlockSpec((tp, pg), lambda i, k: (r0 + i, ub + k)),
                  pl.BlockSpec((1, tp, tp), lambda i, k: (k, 0, 0)),
                  pl.BlockSpec((1, tp, LANES), lambda i, k: (k, 0, 0)),
                  pl.BlockSpec((None, 1, pg, po), lambda i, k: (l, k, 0, 0)),
                  pl.BlockSpec((1, po), lambda i, k: (0, k)),
                  pl.BlockSpec((tp, po), lambda i, k: (r0 + i, gb + k)),
                  pl.BlockSpec((tp, po), lambda i, k: (r0 + i, k))],
        out_specs=pl.BlockSpec((tp, po), lambda i, k: (r0 + i, k)),
        out_shape=jax.ShapeDtypeStruct(merged.shape, merged.dtype),
        input_output_aliases={6: 0},
        compiler_params=_params("parallel", "arbitrary"),
        name="pool_branch",
    )(p, band, invc, w_pool, pool_scale.reshape(1, d), p, merged)


def _confconv_kernel(a_ref, b_ref, w_ref, bias_ref, o_ref, scr, *, seq, taps, stride, chunk, pad):
    lanes = a_ref.shape[1]
    first = pad - (taps // 2) * stride
    scr[pl.ds(0, pad), :] = jnp.zeros((pad, lanes), F32)
    scr[pl.ds(pad + seq, pad), :] = jnp.zeros((pad, lanes), F32)
    scr[pl.ds(pad, seq), :] = a_ref[...].astype(F32) * jax.nn.sigmoid(b_ref[...].astype(F32))

    def do_chunk(base):
        acc = jnp.broadcast_to(bias_ref[...], (chunk, lanes))
        for k in range(taps):
            acc = acc + w_ref[k:k + 1, :] * scr[pl.ds(base + first + k * stride, chunk), :]
        o_ref[pl.ds(base, chunk), :] = acc

    n_chunks = seq // chunk
    if n_chunks == 1:
        do_chunk(0)
    else:
        def body(ci, carry):
            do_chunk(pl.multiple_of(ci * chunk, chunk))
            return carry
        lax.fori_loop(0, n_chunks, body, 0)


def _confconv(p, w, b, a_off, row0, seq, n_seq, stride):
    taps, cw = w.shape
    tc = LANES
    chunk = min(seq, 256)
    pad_alloc = -(-((taps // 2) * stride) // 8) * 8
    ab = a_off // tc
    bb = (a_off + cw) // tc
    r0 = row0 // seq
    kern = functools.partial(_confconv_kernel, seq=seq, taps=taps, stride=stride, chunk=chunk,
                             pad=pad_alloc)
    return pl.pallas_call(
        kern,
        grid=(n_seq, cw // tc),
        in_specs=[pl.BlockSpec((seq, tc), lambda s, j: (r0 + s, ab + j)),
                  pl.BlockSpec((seq, tc), lambda s, j: (r0 + s, bb + j)),
                  pl.BlockSpec((taps, tc), lambda s, j: (0, j)),
                  pl.BlockSpec((1, tc), lambda s, j: (0, j))],
        out_specs=pl.BlockSpec((seq, tc), lambda s, j: (s, j)),
        out_shape=jax.ShapeDtypeStruct((n_seq * seq, cw), F32),
        scratch_shapes=[pltpu.VMEM((seq + 2 * pad_alloc, tc), F32)],
        compiler_params=_params("parallel", "parallel"),
        name="conf_conv",
    )(p, p, w, b.reshape(1, cw))


def _ln_silu_kernel(xl_ref, xc_ref, w_ref, b_ref, o_ref, *, n_lat):
    def run(x_ref):
        x = x_ref[...]
        mu = jnp.mean(x, axis=-1, keepdims=True)
        xc = x - mu
        var = jnp.mean(xc * xc, axis=-1, keepdims=True)
        y = xc * lax.rsqrt(var + EPS) * w_ref[...] + b_ref[...]
        o_ref[...] = _silu(y).astype(o_ref.dtype)

    is_lat = pl.program_id(0) < n_lat
    pl.when(is_lat)(lambda: run(xl_ref))
    pl.when(jnp.logical_not(is_lat))(lambda: run(xc_ref))


def _ln_silu(v_lat, v_ctx, w, b):
    cw = v_lat.shape[1]
    tm = ELT_TILE
    n_lat = v_lat.shape[0] // tm
    t = v_lat.shape[0] + v_ctx.shape[0]
    xl_spec, xc_spec = _two_source_specs((tm, cw), n_lat, ())
    return pl.pallas_call(
        functools.partial(_ln_silu_kernel, n_lat=n_lat),
        grid=(t // tm,),
        in_specs=[xl_spec, xc_spec,
                  pl.BlockSpec((1, cw), lambda i: (0, 0)),
                  pl.BlockSpec((1, cw), lambda i: (0, 0))],
        out_specs=pl.BlockSpec((tm, cw), lambda i: (i, 0)),
        out_shape=jax.ShapeDtypeStruct((t, cw), BF16),
        compiler_params=_params("parallel"),
        name="conf_ln_silu",
    )(v_lat, v_ctx, w.reshape(1, cw), b.reshape(1, cw))


def _top1(rows):
    best, idx = rows[0], jnp.zeros(rows[0].shape, jnp.int32)
    for j in range(1, len(rows)):
        better = rows[j] > best
        idx = jnp.where(better, j, idx)
        best = jnp.where(better, rows[j], best)
    return best, idx


def _router_kernel(x_ref, w_ref, sc_ref, sh_ref, wr_ref, rb_ref, h_ref, sel_ref, *, n_experts):
    h2 = _prenorm_math(x_ref[...], w_ref[...], sc_ref[0], sh_ref[0])
    h_hi, h_lo = _split_bf16(h2, 2)
    h_ref[...] = _pack_bf16_pairs(h_hi.astype(F32))
    w_hi, w_lo = _split_bf16(wr_ref[...], 2)
    nt = (((1,), (1,)), ((), ()))

    def dg(a, b):
        return lax.dot_general(a, b, nt, preferred_element_type=F32)

    logits = dg(w_hi, h_hi) + dg(w_hi, h_lo) + dg(w_lo, h_hi)
    scores = jax.nn.sigmoid(logits)
    sel = scores + rb_ref[:, 0:1]
    per_group = n_experts // N_EXPERT_GROUPS
    sel_rows = [sel[e:e + 1, :] for e in range(n_experts)]
    score_rows = [scores[e:e + 1, :] for e in range(n_experts)]
    neg = jnp.full(sel_rows[0].shape, -jnp.inf, F32)

    group_scores = []
    for g in range(N_EXPERT_GROUPS):
        rows = sel_rows[g * per_group:(g + 1) * per_group]
        m1, i1 = _top1(rows)
        m2, _ = _top1([jnp.where(i1 == j, neg, rows[j]) for j in range(per_group)])
        group_scores.append(m1 + m2)
    _, best_group = _top1(group_scores)

    masked = [jnp.where(best_group == e // per_group, sel_rows[e], neg) for e in range(n_experts)]
    _, e1 = _top1(masked)
    _, e2 = _top1([jnp.where(e1 == e, neg, masked[e]) for e in range(n_experts)])
    s1 = sum(jnp.where(e1 == e, score_rows[e], 0.0) for e in range(n_experts))
    s2 = sum(jnp.where(e2 == e, score_rows[e], 0.0) for e in range(n_experts))
    tot = s1 + s2
    sel_ref[...] = jnp.zeros(sel_ref.shape, F32)
    sel_ref[0:1, :] = e1.astype(F32)
    sel_ref[1:2, :] = e2.astype(F32)
    sel_ref[2:3, :] = s1 / tot
    sel_ref[3:4, :] = s2 / tot


def _router(xa, w, mods, sc_col, sh_col, seq, n_batch, w_router, router_bias):
    t, d = xa.shape
    n_experts = w_router.shape[1]
    tm = ELT_TILE
    wr_t = w_router.astype(F32).T
    rb = jnp.broadcast_to(router_bias.astype(F32)[:, None], (n_experts, LANES))
    return pl.pallas_call(
        functools.partial(_router_kernel, n_experts=n_experts),
        grid=(t // tm,),
        in_specs=[pl.BlockSpec((tm, d), lambda i: (i, 0)),
                  pl.BlockSpec((1, d), lambda i: (0, 0)),
                  _mod_spec(d, sc_col, seq, n_batch + 1, tm),
                  _mod_spec(d, sh_col, seq, n_batch + 1, tm),
                  pl.BlockSpec((n_experts, d), lambda i: (0, 0)),
                  pl.BlockSpec((n_experts, LANES), lambda i: (0, 0))],
        out_specs=[pl.BlockSpec((tm, d // 2), lambda i: (i, 0)),
                   pl.BlockSpec((8, tm), lambda i: (0, i))],
        out_shape=[jax.ShapeDtypeStruct((t, d // 2), jnp.uint32),
                   jax.ShapeDtypeStruct((8, t), F32)],
        compiler_params=_params("parallel"),
        name="moe_prenorm_router",
    )(xa, w.reshape(1, d), mods, mods, wr_t, rb)


def _row_copy(src_hbm, row, dst, slot, sem):
    return pltpu.make_async_copy(src_hbm.at[pl.ds(row, 1), :], dst.at[pl.ds(slot, 1), :], sem)


def _gather_rows(src_hbm, idx_ref, dst, sem, n_rows):
    def issue(r, carry):
        _row_copy(src_hbm, idx_ref[0, 0, r], dst, r, sem).start()
        return carry
    lax.fori_loop(0, n_rows, issue, 0, unroll=GATHER_UNROLL)


def _wait_rows(src_hbm, dst, sem, n_rows):
    pltpu.make_async_copy(src_hbm.at[pl.ds(0, n_rows), :], dst, sem).wait()


def _moe_kernel(te_ref, nv_ref, tok_ref, tok_next_ref, h_hbm, wg_ref, wu_ref, wd_ref, o_ref,
                xbuf, xb, hg, hu, sem, *, tm):
    i = pl.program_id(0)
    f = pl.program_id(1)
    last = pl.num_programs(1) - 1
    n_valid = nv_ref[0]
    valid = i < n_valid
    n_slices, _, kc = xb.shape
    per_half = n_slices // 2
    half = o_ref.shape[1]

    @pl.when(f == 0)
    def _():
        @pl.when(i == 0)
        def _():
            _gather_rows(h_hbm, tok_ref, xbuf, sem, tm)

        @pl.when(valid)
        def _():
            _wait_rows(h_hbm, xbuf, sem, tm)
            lo, hi = _unpack_bf16_pairs(xbuf[...])
            for s in range(n_slices):
                src = lo if s < per_half else hi
                off = (s % per_half) * kc
                xb[s] = src[:, off:off + kc].astype(BF16)

        @pl.when(i + 1 < n_valid)
        def _():
            _gather_rows(h_hbm, tok_next_ref, xbuf, sem, tm)

    @pl.when(valid)
    def _():
        x = xb[f]
        pg = _dot(x, wg_ref[...])
        pu = _dot(x, wu_ref[...])

        @pl.when(f == 0)
        def _():
            hg[...] = pg
            hu[...] = pu

        @pl.when(jnp.logical_and(f > 0, f < last))
        def _():
            hg[...] += pg
            hu[...] += pu

        @pl.when(f == last)
        def _():
            hid = (_silu(hg[...] + pg) * (hu[...] + pu)).astype(BF16)
            lo = _dot(hid, wd_ref[:, :half]).astype(BF16).astype(F32)
            hi = _dot(hid, wd_ref[:, half:]).astype(BF16).astype(F32)
            o_ref[...] = _pack_halves(lo, hi)

    @pl.when(jnp.logical_and(f == last, jnp.logical_not(valid)))
    def _():
        o_ref[...] = jnp.zeros(o_ref.shape, o_ref.dtype)


def _moe(h2p, tile_expert, n_valid, row_token, wg, wu, wd, l):
    half = h2p.shape[1]
    d = 2 * half
    ff = wg.shape[3]
    kc = d // MOE_K_SPLIT
    assert MOE_K_SPLIT % 2 == 0 and MOE_K_SPLIT >= 2
    tm = MOE_TILE
    n_tiles = tile_expert.shape[0]
    tok = row_token.reshape(n_tiles, 1, tm)
    w_in_spec = pl.BlockSpec((None, None, kc, ff), lambda i, f, te, nv: (l, te[i], f, 0))
    grid_spec = pltpu.PrefetchScalarGridSpec(
        num_scalar_prefetch=2,
        grid=(n_tiles, MOE_K_SPLIT),
        in_specs=[pl.BlockSpec((1, 1, tm), lambda i, f, te, nv: (i, 0, 0), memory_space=pltpu.SMEM),
                  pl.BlockSpec((1, 1, tm), lambda i, f, te, nv: (jnp.minimum(i + 1, n_tiles - 1), 0, 0),
                               memory_space=pltpu.SMEM),
                  pl.BlockSpec(memory_space=pl.ANY),
                  w_in_spec, w_in_spec,
                  pl.BlockSpec((None, None, ff, d), lambda i, f, te, nv: (l, te[i], 0, 0),
                               pipeline_mode=pl.Buffered(1))],
        out_specs=pl.BlockSpec((tm, half), lambda i, f, te, nv: (i, 0)),
        scratch_shapes=[pltpu.VMEM((tm, half), jnp.uint32), pltpu.VMEM((MOE_K_SPLIT, tm, kc), BF16),
                        pltpu.VMEM((tm, ff), F32), pltpu.VMEM((tm, ff), F32),
                        pltpu.SemaphoreType.DMA(())],
    )
    return pl.pallas_call(
        functools.partial(_moe_kernel, tm=tm),
        grid_spec=grid_spec,
        out_shape=jax.ShapeDtypeStruct((n_tiles * tm, half), jnp.uint32),
        compiler_params=_params("arbitrary", "arbitrary"),
        name="moe_experts",
    )(tile_expert, n_valid, tok, tok, h2p, wg, wu, wd)


def _combine_kernel(p0_ref, p1_ref, p0n_ref, p1n_ref, x_ref, gm_ref, gw_ref, nw_ref, sc_ref, sh_ref,
                    ys_hbm, *rest, tc, final):
    outs, (buf, sem) = rest[:-2], rest[-2:]
    i = pl.program_id(0)
    slot = lax.rem(i, 2)
    half = buf.shape[3]

    def gather(pa_ref, pb_ref, s):
        _gather_rows(ys_hbm, pa_ref, buf.at[s, 0], sem.at[s], tc)
        _gather_rows(ys_hbm, pb_ref, buf.at[s, 1], sem.at[s], tc)

    @pl.when(i == 0)
    def _():
        gather(p0_ref, p1_ref, 0)

    @pl.when(i + 1 < pl.num_programs(0))
    def _():
        gather(p0n_ref, p1n_ref, 1 - slot)

    _wait_rows(ys_hbm, buf.at[slot, 0], sem.at[slot], tc)
    _wait_rows(ys_hbm, buf.at[slot, 1], sem.at[slot], tc)
    lo0, hi0 = _unpack_bf16_pairs(buf[slot, 0])
    lo1, hi1 = _unpack_bf16_pairs(buf[slot, 1])
    w0 = gw_ref[:, 0:1]
    w1 = gw_ref[:, LANES:LANES + 1]
    g = gm_ref[0]
    new_lo = x_ref[:, :half] + g[:, :half] * (w0 * lo0 + w1 * lo1)
    new_hi = x_ref[:, half:] + g[:, half:] * (w0 * hi0 + w1 * hi1)
    ms = (jnp.sum(new_lo * new_lo, axis=-1, keepdims=True)
          + jnp.sum(new_hi * new_hi, axis=-1, keepdims=True)) * (1.0 / (2 * half))
    inv = lax.rsqrt(ms + EPS)
    nw = nw_ref[...]
    if final:
        o_ref, = outs
        o_ref[:, :half] = new_lo * inv * nw[:, :half]
        o_ref[:, half:] = new_hi * inv * nw[:, half:]
    else:
        x_out_ref, h_ref = outs
        x_out_ref[:, :half] = new_lo
        x_out_ref[:, half:] = new_hi
        sc = 1.0 + sc_ref[0]
        sh = sh_ref[0]
        h_ref[:, :half] = (new_lo * inv * nw[:, :half] * sc[:, :half] + sh[:, :half]).astype(h_ref.dtype)
        h_ref[:, half:] = (new_hi * inv * nw[:, half:] * sc[:, half:] + sh[:, half:]).astype(h_ref.dtype)


def _combine(xa, ys, pos0, pos1, gate_cols, mods, mod_col, seq, n_batch, norm_w, next_mods, final,
             n_rows):
    d = xa.shape[1]
    half = d // 2
    tc = ELT_TILE
    n = n_rows // tc
    p0 = pos0[:n_rows].reshape(n, 1, tc)
    p1 = pos1[:n_rows].reshape(n, 1, tc)
    cur = functools.partial(pl.BlockSpec, (1, 1, tc), lambda i: (i, 0, 0), memory_space=pltpu.SMEM)
    nxt = functools.partial(pl.BlockSpec, (1, 1, tc), lambda i: (jnp.minimum(i + 1, n - 1), 0, 0),
                            memory_space=pltpu.SMEM)
    row_spec = pl.BlockSpec((tc, d), lambda i: (i, 0))
    if final:
        out_specs = [row_spec]
        out_shape = [jax.ShapeDtypeStruct((n_rows, d), F32)]
        aliases = {}
    else:
        out_specs = [row_spec, row_spec]
        out_shape = [jax.ShapeDtypeStruct((n_rows, d), F32), jax.ShapeDtypeStruct((n_rows, d), BF16)]
        aliases = {4: 0}
    return pl.pallas_call(
        functools.partial(_combine_kernel, tc=tc, final=final),
        grid=(n,),
        in_specs=[cur(), cur(), nxt(), nxt(),
                  row_spec,
                  _mod_spec(d, mod_col, seq, n_batch + 1, tc),
                  pl.BlockSpec((tc, 2 * LANES), lambda i: (i, 0)),
                  pl.BlockSpec((1, d), lambda i: (0, 0)),
                  _mod_spec(d, 1, seq, n_batch + 1, tc),
                  _mod_spec(d, 0, seq, n_batch + 1, tc),
                  pl.BlockSpec(memory_space=pl.ANY)],
        out_specs=out_specs,
        out_shape=out_shape,
        scratch_shapes=[pltpu.VMEM((2, 2, tc, half), jnp.uint32), pltpu.SemaphoreType.DMA((2,))],
        input_output_aliases=aliases,
        compiler_params=_params("arbitrary"),
        name="moe_combine",
    )(p0, p1, p0, p1, xa, mods, gate_cols, norm_w.reshape(1, d), next_mods, next_mods, ys)


def _routing_tables(sel, n_experts, tm):
    t = sel.shape[1]
    e = sel[0:2].astype(jnp.int32).reshape(-1)
    onehot = (e[:, None] == jnp.arange(n_experts, dtype=jnp.int32)[None, :]).astype(jnp.int32)
    incl = jnp.cumsum(onehot, axis=0)
    rank = jnp.sum((incl - onehot) * onehot, axis=1)
    counts = incl[-1]
    tiles = (counts + tm - 1) // tm
    tile_end = jnp.cumsum(tiles)
    tile_start = tile_end - tiles
    pos = tile_start[e] * tm + rank
    n_tiles = (2 * t) // tm + n_experts
    token = jnp.tile(jnp.arange(t, dtype=jnp.int32), 2)
    row_token = jnp.zeros((n_tiles * tm,), jnp.int32).at[pos].set(token)
    n_valid = tile_end[-1]
    tile_ids = jnp.minimum(jnp.arange(n_tiles, dtype=jnp.int32), n_valid - 1)
    tile_expert = jnp.sum((tile_ids[:, None] >= tile_end[None, :]).astype(jnp.int32), axis=1)
    gate_cols = jnp.broadcast_to(sel[2:4].T[:, :, None], (t, 2, LANES)).reshape(t, 2 * LANES)
    return tile_expert, n_valid.reshape(1), row_token, pos[:t], pos[t:], gate_cols


def kernel(x, c, ctx, c_ctx, w_ada, b_ada, norm1_w, norm2_w, w_in, ssd_conv_w, ssd_conv_b,
           ssd_A_log, ssd_dt_bias, ssd_D, ssd_norm_w, w_ssd_out, w_pool, pool_scale,
           conf_conv_w, conf_conv_b, conf_ln_w, conf_ln_b, w_conf_out, w_out, w_router,
           router_bias, w_exp_gate, w_exp_up, w_exp_down, final_norm_w):
    n_batch, seq, d = x.shape
    ctx_len = ctx.shape[1]
    depth = w_in.shape[0]
    inner = w_ssd_out.shape[1]
    heads = ssd_A_log.shape[-1]
    xbc_w = ssd_conv_w.shape[-1]
    gn = (xbc_w - inner) // 2
    n_groups = gn // SSD_STATE
    hpg = heads // n_groups
    geo = (inner, n_groups, SSD_STATE, hpg, SSD_HEAD_DIM)
    pool_w = w_pool.shape[1] * w_pool.shape[2]
    conf_w = conf_conv_w.shape[-1]
    n_experts = w_router.shape[1]
    assert heads <= LANES and hpg * SSD_HEAD_DIM * n_groups == inner
    assert seq % GRID_W == 0 and seq % ROW_TILE == 0 and ctx_len % ELT_TILE == 0

    off_dt = xbc_w
    off_z = off_dt + 2 * heads
    off_pool = off_z + inner
    off_conf = off_pool + pool_w
    off_gate = off_conf + 2 * conf_w
    p_z = xbc_w
    p_pool = p_z + inner
    p_conf = p_pool + pool_w
    p_gate = p_conf + 2 * conf_w

    t_lat = n_batch * seq
    t_ctx = n_batch * ctx_len
    t_all = t_lat + t_ctx
    assert t_all % ROW_TILE == 0 and t_ctx % ROW_TILE == 0 and t_ctx <= seq
    x_rows = x.reshape(t_lat, d)
    ctx_rows = ctx.reshape(t_ctx, d)

    c_rows = jnp.zeros((BF16_SUBLANES, d), F32).at[:n_batch].set(c).at[n_batch].set(c_ctx)
    h0 = jnp.zeros((n_batch, 2, n_groups, SSD_STATE, hpg * SSD_HEAD_DIM), F32)

    w_main = jnp.concatenate([w_in[:, :, :off_dt], w_in[:, :, off_z:]], axis=2).astype(BF16)
    w_dt = jnp.zeros((depth, d, 2 * LANES), F32)
    w_dt = w_dt.at[:, :, :heads].set(w_in[:, :, off_dt:off_dt + heads])
    w_dt = w_dt.at[:, :, LANES:LANES + heads].set(w_in[:, :, off_dt + heads:off_z]).astype(BF16)
    w_ssd_out_b = w_ssd_out.astype(BF16)
    w_pool_b = w_pool.astype(BF16)
    w_conf_out_b = w_conf_out.astype(BF16)
    w_out_b = w_out.astype(BF16)
    wg_b = w_exp_gate.astype(BF16)
    wu_b = w_exp_up.astype(BF16)
    wd_b = w_exp_down.astype(BF16)

    all_mods = [_ada(c_rows, w_ada, b_ada, l).reshape(BF16_SUBLANES, 1, 6 * d) for l in range(depth)]
    xa = None
    h = _prenorm(x_rows, ctx_rows, norm1_w[0], all_mods[0], 1, 0, seq, n_batch)
    for l in range(depth):
        mods = all_mods[l]

        p = _mm(h, w_main, l, BF16, name="in_proj")
        dtr = _mm(h, w_dt, l, F32, name="dt_proj")

        xbc_ctx = _seqconv(p, ssd_conv_w[l], ssd_conv_b[l], t_lat, ctx_len, n_batch)
        xbc_lat = _seqconv(p, ssd_conv_w[l], ssd_conv_b[l], 0, seq, n_batch)
        y_ctx, s_ctx = _ssd(xbc_ctx, dtr, t_lat, ssd_A_log[l], ssd_dt_bias[l], ssd_D[l], h0,
                            ctx_len, n_batch, geo)
        y_lat, _ = _ssd(xbc_lat, dtr, 0, ssd_A_log[l], ssd_dt_bias[l], ssd_D[l], s_ctx,
                        seq, n_batch, geo)
        ya = _gate_norm(y_lat, y_ctx, p, p_z, ssd_norm_w[l])
        merged = _mm(ya, w_ssd_out_b, l, BF16, gate=p, gate_off=p_gate, name="ssd_out")

        merged = _pool(p, merged, w_pool_b, l, pool_scale[l], p_pool, p_gate + d, 0, t_lat, GRID_W)
        merged = _pool(p, merged, w_pool_b, l, pool_scale[l], p_pool, p_gate + d, t_lat, t_ctx,
                       ctx_len)

        v_lat = _confconv(p, conf_conv_w[l], conf_conv_b[l], p_conf, 0, seq, n_batch, GRID_W)
        v_ctx = _confconv(p, conf_conv_w[l], conf_conv_b[l], p_conf, t_lat, ctx_len, n_batch, 1)
        ca = _ln_silu(v_lat, v_ctx, conf_ln_w[l], conf_ln_b[l])
        merged = _mm(ca, w_conf_out_b, l, BF16, gate=p, gate_off=p_gate + 2 * d, prev=merged,
                     name="conf_out")

        resid = (x_rows, ctx_rows) if xa is None else xa
        xa = _mm(merged, w_out_b, l, F32, resid=resid, mods=mods, mod_col=2, seq=seq,
                 n_batch=n_batch, name="mixer_out")

        h2p, sel = _router(xa, norm2_w[l], mods, 4, 3, seq, n_batch, w_router, router_bias)
        tile_expert, n_valid, row_token, pos0, pos1, gate_cols = _routing_tables(
            sel, n_experts, MOE_TILE)
        ys = _moe(h2p, tile_expert, n_valid, row_token, wg_b, wu_b, wd_b, l)
        if l + 1 < depth:
            xa, h = _combine(xa, ys, pos0, pos1, gate_cols, mods, 5, seq, n_batch, norm1_w[l + 1],
                             all_mods[l + 1], False, t_all)
        else:
            out, = _combine(xa, ys, pos0, pos1, gate_cols, mods, 5, seq, n_batch, final_norm_w,
                            mods, True, t_lat)
    return out.reshape(n_batch, seq, d)
```

```python
import functools

import numpy as np
import jax
import jax.numpy as jnp
from jax import lax
from jax.experimental import pallas as pl
from jax.experimental.pallas import tpu as pltpu

GRID_W = 64
EPS = 1e-6
SSD_HEAD_DIM = 64
SSD_STATE = 128
SSD_CHUNK = 128
POOL_WINDOWS = (2, 4, 8, 16)
N_EXPERT_GROUPS = 4
N_BRANCHES = 3

LANES = 128
BF16_SUBLANES = 16
VMEM_LIMIT_BYTES = 56 * 1024 * 1024

ROW_TILE = 512
ELT_TILE = 256
MOE_TILE = 512
MOE_K_SPLIT = 4
GATHER_UNROLL = 8

F32 = jnp.float32
BF16 = jnp.bfloat16


def _params(*sem):
    return pltpu.CompilerParams(dimension_semantics=sem, vmem_limit_bytes=VMEM_LIMIT_BYTES)


def _pick_tile(n, *offsets, choices=(1024, 512, 256, 128)):
    for t in choices:
        if n % t == 0 and all(o % t == 0 for o in offsets):
            return t
    raise ValueError(f"no lane tile divides {n} and offsets {offsets}")


def _dot(a, b):
    return jnp.dot(a, b, preferred_element_type=F32)


def _split_bf16(v, parts):
    out = []
    rem = v
    for _ in range(parts):
        hi = rem.astype(BF16)
        out.append(hi)
        rem = rem - hi.astype(F32)
    return out


def _silu(v):
    return v * jax.nn.sigmoid(v)


_HIGH16 = np.uint32(0xFFFF0000)


def _pack_halves(lo, hi):
    return ((lax.bitcast_convert_type(lo, jnp.uint32) >> 16)
            | (lax.bitcast_convert_type(hi, jnp.uint32) & _HIGH16))


def _pack_bf16_pairs(v):
    half = v.shape[1] // 2
    return _pack_halves(v[:, :half], v[:, half:])


def _unpack_bf16_pairs(u):
    lo = lax.bitcast_convert_type(u << 16, F32)
    hi = lax.bitcast_convert_type(u & _HIGH16, F32)
    return lo, hi


def _ada_kernel(c_ref, w_ref, b_ref, o_ref):
    s = _silu(c_ref[...]).astype(BF16)
    o_ref[...] = _dot(s, w_ref[...].astype(BF16)) + b_ref[...]


def _ada(c_rows, w, b, l):
    m, d = c_rows.shape
    n = w.shape[2]
    tn = _pick_tile(n, choices=(512, 256, 128))
    return pl.pallas_call(
        _ada_kernel,
        grid=(n // tn,),
        in_specs=[pl.BlockSpec((m, d), lambda j: (0, 0)),
                  pl.BlockSpec((None, d, tn), lambda j: (l, 0, j)),
                  pl.BlockSpec((None, 1, tn), lambda j: (l, 0, j))],
        out_specs=pl.BlockSpec((m, tn), lambda j: (0, j)),
        out_shape=jax.ShapeDtypeStruct((m, n), F32),
        compiler_params=_params("arbitrary"),
        name="ada_mod",
    )(c_rows, w, b.reshape(b.shape[0], 1, n))


def _prenorm_math(x, w, sc, sh):
    ms = jnp.mean(x * x, axis=-1, keepdims=True)
    y = x * lax.rsqrt(ms + EPS) * w
    return y * (1.0 + sc) + sh


def _two_source_specs(block, n_lat, lead):
    def lat_map(i):
        return lead + (jnp.minimum(i, n_lat - 1), 0)

    def ctx_map(i):
        return lead + (jnp.maximum(i - n_lat, 0), 0)
    return pl.BlockSpec(block, lat_map), pl.BlockSpec(block, ctx_map)


def _prenorm_kernel(xl_ref, xc_ref, w_ref, sc_ref, sh_ref, o_ref, *, n_lat):
    def run(x_ref):
        o_ref[...] = _prenorm_math(x_ref[...], w_ref[...], sc_ref[0], sh_ref[0]).astype(o_ref.dtype)

    is_lat = pl.program_id(0) < n_lat
    pl.when(is_lat)(lambda: run(xl_ref))
    pl.when(jnp.logical_not(is_lat))(lambda: run(xc_ref))


def _mod_spec(d, col, rows_per_mod, n_mod, tm):
    tiles_per_mod = rows_per_mod // tm
    return pl.BlockSpec((1, 1, d), lambda i: (jnp.minimum(i // tiles_per_mod, n_mod - 1), 0, col))


def _prenorm(x_lat, x_ctx, w, mods, sc_col, sh_col, seq, n_batch):
    d = x_lat.shape[1]
    tm = ELT_TILE
    n_lat = x_lat.shape[0] // tm
    t = x_lat.shape[0] + x_ctx.shape[0]
    xl_spec, xc_spec = _two_source_specs((tm, d), n_lat, ())
    return pl.pallas_call(
        functools.partial(_prenorm_kernel, n_lat=n_lat),
        grid=(t // tm,),
        in_specs=[xl_spec, xc_spec,
                  pl.BlockSpec((1, d), lambda i: (0, 0)),
                  _mod_spec(d, sc_col, seq, n_batch + 1, tm),
                  _mod_spec(d, sh_col, seq, n_batch + 1, tm)],
        out_specs=pl.BlockSpec((tm, d), lambda i: (i, 0)),
        out_shape=jax.ShapeDtypeStruct((t, d), BF16),
        compiler_params=_params("parallel"),
        name="prenorm",
    )(x_lat, x_ctx, w.reshape(1, d), mods, mods)


def _mm_plain_kernel(a_ref, w_ref, o_ref):
    o_ref[...] = _dot(a_ref[...], w_ref[...]).astype(o_ref.dtype)


def _mm_gate_kernel(a_ref, w_ref, g_ref, o_ref):
    acc = _dot(a_ref[...], w_ref[...])
    o_ref[...] = (jax.nn.sigmoid(g_ref[...].astype(F32)) * acc).astype(o_ref.dtype)


def _mm_gate_acc_kernel(a_ref, w_ref, g_ref, prev_ref, o_ref):
    acc = _dot(a_ref[...], w_ref[...])
    o_ref[...] = (prev_ref[...].astype(F32)
                  + jax.nn.sigmoid(g_ref[...].astype(F32)) * acc).astype(o_ref.dtype)


def _mm_resid_kernel(a_ref, w_ref, x_ref, gm_ref, o_ref):
    acc = _dot(a_ref[...], w_ref[...])
    o_ref[...] = x_ref[...] + gm_ref[0] * acc


def _mm_resid2_kernel(a_ref, w_ref, xl_ref, xc_ref, gm_ref, o_ref, *, n_lat):
    acc = _dot(a_ref[...], w_ref[...])
    is_lat = pl.program_id(1) < n_lat

    @pl.when(is_lat)
    def _():
        o_ref[...] = xl_ref[...] + gm_ref[0] * acc

    @pl.when(jnp.logical_not(is_lat))
    def _():
        o_ref[...] = xc_ref[...] + gm_ref[0] * acc


def _mm(a, w, l, out_dtype, *, gate=None, gate_off=0, prev=None, resid=None, mods=None,
        mod_col=0, seq=0, n_batch=0, name="mm"):
    t, k = a.shape
    n = w.shape[2]
    tm = ROW_TILE
    tn = _pick_tile(n, gate_off)
    grid = (n // tn, t // tm)
    a_spec = pl.BlockSpec((tm, k), lambda j, i: (i, 0))
    w_spec = pl.BlockSpec((None, k, tn), lambda j, i: (l, 0, j))
    o_spec = pl.BlockSpec((tm, tn), lambda j, i: (i, j))
    goff = gate_off // tn
    g_spec = pl.BlockSpec((tm, tn), lambda j, i: (i, goff + j))
    aliases = {}
    if resid is not None:
        tiles_per_mod = seq // tm
        mcol = mod_col * (n // tn)
        gm_spec = pl.BlockSpec(
            (1, 1, tn), lambda j, i: (jnp.minimum(i // tiles_per_mod, n_batch), 0, mcol + j))
        if isinstance(resid, tuple):
            x_lat, x_ctx = resid
            n_lat = x_lat.shape[0] // tm
            xl_spec = pl.BlockSpec((tm, tn), lambda j, i: (jnp.minimum(i, n_lat - 1), j))
            xc_spec = pl.BlockSpec((tm, tn), lambda j, i: (jnp.maximum(i - n_lat, 0), j))
            kern = functools.partial(_mm_resid2_kernel, n_lat=n_lat)
            ins, specs = (a, w, x_lat, x_ctx, mods), [a_spec, w_spec, xl_spec, xc_spec, gm_spec]
        else:
            kern, ins, specs = _mm_resid_kernel, (a, w, resid, mods), [a_spec, w_spec, o_spec, gm_spec]
            aliases = {2: 0}
    elif prev is not None:
        kern, ins, specs = _mm_gate_acc_kernel, (a, w, gate, prev), [a_spec, w_spec, g_spec, o_spec]
        aliases = {3: 0}
    elif gate is not None:
        kern, ins, specs = _mm_gate_kernel, (a, w, gate), [a_spec, w_spec, g_spec]
    else:
        kern, ins, specs = _mm_plain_kernel, (a, w), [a_spec, w_spec]
    return pl.pallas_call(
        kern,
        grid=grid,
        in_specs=specs,
        out_specs=o_spec,
        out_shape=jax.ShapeDtypeStruct((t, n), out_dtype),
        input_output_aliases=aliases,
        compiler_params=_params("parallel", "arbitrary"),
        name=name,
    )(*ins)


def _in_proj_kernel(a_ref, w_ref, wn_ref, o_ref, wb, *, n_plain, shift, rows):
    j = pl.program_id(0)
    k, tn = wb.shape

    @pl.when(pl.program_id(1) == 0)
    def _():
        @pl.when(j < n_plain)
        def _():
            for r in range(0, k, rows):
                wb[pl.ds(r, rows), :] = w_ref[pl.ds(r, rows), :].astype(BF16)

        @pl.when(j >= n_plain)
        def _():
            for r in range(0, k, rows):
                both = jnp.concatenate([w_ref[pl.ds(r, rows), :], wn_ref[pl.ds(r, rows), :]], axis=1)
                wb[pl.ds(r, rows), :] = both[:, shift:shift + tn].astype(BF16)

    o_ref[...] = _dot(a_ref[...], wb[...]).astype(o_ref.dtype)


def _in_proj(a, w_in, l, n_plain_cols, shift, n_out):
    t, k = a.shape
    tm = ROW_TILE
    tn = _pick_tile(n_out, n_plain_cols)
    assert 0 < shift < LANES and w_in.shape[2] == n_out + shift
    per = tn // LANES
    kern = functools.partial(_in_proj_kernel, n_plain=n_plain_cols // tn, shift=shift, rows=256)
    return pl.pallas_call(
        kern,
        grid=(n_out // tn, t // tm),
        in_specs=[pl.BlockSpec((tm, k), lambda j, i: (i, 0)),
                  pl.BlockSpec((None, k, tn), lambda j, i: (l, 0, j), pipeline_mode=pl.Buffered(1)),
                  pl.BlockSpec((None, k, LANES), lambda j, i: (l, 0, per * (j + 1)),
                               pipeline_mode=pl.Buffered(1))],
        out_specs=pl.BlockSpec((tm, tn), lambda j, i: (i, j)),
        out_shape=jax.ShapeDtypeStruct((t, n_out), BF16),
        scratch_shapes=[pltpu.VMEM((k, tn), BF16)],
        compiler_params=_params("arbitrary", "arbitrary"),
        name="in_proj",
    )(a, w_in, w_in)


def _seqconv_kernel(prev_ref, cur_ref, next_ref, w_ref, b_ref, o_ref, scr, *, ts, taps):
    i = pl.program_id(1)
    n = pl.num_programs(1)
    h = BF16_SUBLANES
    scr[pl.ds(0, h), :] = jnp.where(i > 0, prev_ref[...].astype(F32), 0.0)
    scr[pl.ds(h, ts), :] = cur_ref[...].astype(F32)
    scr[pl.ds(h + ts, h), :] = jnp.where(i < n - 1, next_ref[...].astype(F32), 0.0)
    acc = jnp.broadcast_to(b_ref[...], (ts, b_ref.shape[1]))
    for k in range(taps):
        acc = acc + w_ref[k:k + 1, :] * scr[pl.ds(h - taps // 2 + k, ts), :]
    o_ref[...] = _silu(acc).astype(o_ref.dtype)


def _seqconv(p, w, b, row0, seq, n_seq):
    taps, width = w.shape
    ts = min(seq, 512)
    tc = _pick_tile(width, choices=(512, 256, 128))
    h = BF16_SUBLANES
    tiles = seq // ts
    cur0 = row0 // ts
    halo0 = row0 // h
    per_tile = ts // h
    last_halo = (row0 + n_seq * seq) // h - 1

    def prev_map(s, i, j):
        return (jnp.maximum(halo0 + (s * tiles + i) * per_tile - 1, 0), j)

    def next_map(s, i, j):
        return (jnp.minimum(halo0 + (s * tiles + i + 1) * per_tile, last_halo), j)

    return pl.pallas_call(
        functools.partial(_seqconv_kernel, ts=ts, taps=taps),
        grid=(n_seq, tiles, width // tc),
        in_specs=[pl.BlockSpec((h, tc), prev_map),
                  pl.BlockSpec((ts, tc), lambda s, i, j: (cur0 + s * tiles + i, j)),
                  pl.BlockSpec((h, tc), next_map),
                  pl.BlockSpec((taps, tc), lambda s, i, j: (0, j)),
                  pl.BlockSpec((1, tc), lambda s, i, j: (0, j))],
        out_specs=pl.BlockSpec((ts, tc), lambda s, i, j: (s * tiles + i, j)),
        out_shape=jax.ShapeDtypeStruct((n_seq * seq, width), BF16),
        scratch_shapes=[pltpu.VMEM((ts + 2 * h, tc), F32)],
        compiler_params=_params("parallel", "parallel", "parallel"),
        name="ssd_seqconv",
    )(p, p, p, w, b.reshape(1, width))


def _ssd_kernel(xbc_ref, dt_ref, alog_ref, dtb_ref, dsk_ref, tri_ref, exp_ref, h0_ref,
                y_ref, hfin_ref, st, *, inner, n_groups, n_state, heads_per_group, head_dim):
    c = pl.program_id(2)
    q = dt_ref.shape[0]
    rp = heads_per_group * head_dim
    gn = n_groups * n_state

    @pl.when(c == 0)
    def _():
        st[...] = h0_ref[0, 0]

    x = dt_ref[...] + dtb_ref[0]
    dt = jnp.maximum(x, 0.0) + jnp.log1p(jnp.exp(-jnp.abs(x)))
    a = dt * (-jnp.exp(alog_ref[0]))
    tri = tri_ref[0]
    tri_b = tri.astype(BF16)
    acs = sum(_dot(tri_b, part) for part in _split_bf16(a, 3))
    atot = jnp.sum(a, axis=0, keepdims=True)
    acs_t = acs.T
    stack = jnp.concatenate(
        [dt, jnp.exp(acs), jnp.exp(atot - acs), jnp.broadcast_to(jnp.exp(atot), (8, LANES))], axis=0)
    expand = exp_ref[...]
    wide = sum(_dot(part, expand) for part in _split_bf16(stack, 2))
    dt_x = wide[0:q]
    eacs_x = wide[q:2 * q]
    edec_x = wide[2 * q:3 * q]
    etot_x = wide[3 * q:3 * q + 1]

    xs = xbc_ref[:, 0:inner].astype(F32)
    xdt = xs * dt_x
    xdt_b = xdt.astype(BF16)
    xw_b = (xdt * edec_x).astype(BF16)
    mask = tri > 0.0
    lane_head = lax.broadcasted_iota(jnp.int32, (1, rp), 1) // head_dim
    dsk = dsk_ref[0]

    for g in range(n_groups):
        bg = xbc_ref[:, inner + g * n_state: inner + (g + 1) * n_state]
        cg = xbc_ref[:, inner + gn + g * n_state: inner + gn + (g + 1) * n_state]
        bg_t = bg.astype(F32).T.astype(BF16)
        cb = _dot(cg, bg_t)
        sl = slice(g * rp, (g + 1) * rp)
        chunk_state = _dot(bg_t, xw_b[:, sl])
        st_g = st[g]
        y_off = _dot(cg, st_g.astype(BF16)) * eacs_x[:, sl]
        ms, xm = [], []
        for r in range(heads_per_group):
            hh = g * heads_per_group + r
            seg = acs[:, hh:hh + 1] - acs_t[hh:hh + 1, :]
            decay = jnp.exp(jnp.where(mask, seg, -jnp.inf))
            ms.append((cb * decay).astype(BF16))
            xm.append(jnp.where(lane_head == r, xdt_b[:, sl], jnp.zeros_like(xdt_b[:, sl])))
        y_diag = _dot(jnp.concatenate(ms, axis=1), jnp.concatenate(xm, axis=0))
        y = y_diag + y_off + dsk[:, sl] * xs[:, sl]
        y_ref[0, :, sl] = y.astype(y_ref.dtype)
        st[g] = st_g * etot_x[:, sl] + chunk_state

    @pl.when(c == pl.num_programs(2) - 1)
    def _():
        hfin_ref[0, 0] = st[...]


def _ssd(xbc, dtr, dt_row0, a_log, dt_bias, d_skip, h0, seq, n_seq, geo):
    inner, n_groups, n_state, hpg, head_dim = geo
    heads = n_groups * hpg
    q = SSD_CHUNK
    nc = seq // q
    rp = hpg * head_dim
    xbc_w = xbc.shape[1]

    def pad_heads(v):
        return jnp.pad(v.astype(F32), ((0, 0), (0, LANES - heads))).reshape(2, 1, LANES)

    dsk = jnp.repeat(d_skip.astype(F32), head_dim, axis=-1).reshape(2, 1, inner)
    low = np.tril(np.ones((q, q), np.float32))
    tri = jnp.asarray(np.stack([low, low.T]))
    expand = np.zeros((LANES, inner), np.float32)
    for hh in range(heads):
        expand[hh, hh * head_dim:(hh + 1) * head_dim] = 1.0
    expand = jnp.asarray(expand, dtype=BF16)
    dt_blk0 = dt_row0 // q

    def chunk(s, d, c):
        return s * nc + c + d * (nc - 1 - 2 * c)

    kern = functools.partial(_ssd_kernel, inner=inner, n_groups=n_groups, n_state=n_state,
                             heads_per_group=hpg, head_dim=head_dim)
    y, hfin = pl.pallas_call(
        kern,
        grid=(n_seq, 2, nc),
        in_specs=[pl.BlockSpec((q, xbc_w), lambda s, d, c: (chunk(s, d, c), 0)),
                  pl.BlockSpec((q, LANES), lambda s, d, c: (dt_blk0 + chunk(s, d, c), d)),
                  pl.BlockSpec((1, 1, LANES), lambda s, d, c: (d, 0, 0)),
                  pl.BlockSpec((1, 1, LANES), lambda s, d, c: (d, 0, 0)),
                  pl.BlockSpec((1, 1, inner), lambda s, d, c: (d, 0, 0)),
                  pl.BlockSpec((1, q, q), lambda s, d, c: (d, 0, 0)),
                  pl.BlockSpec((LANES, inner), lambda s, d, c: (0, 0)),
                  pl.BlockSpec((1, 1, n_groups, n_state, rp), lambda s, d, c: (s, d, 0, 0, 0))],
        out_specs=[pl.BlockSpec((1, q, inner), lambda s, d, c: (d, chunk(s, d, c), 0)),
                   pl.BlockSpec((1, 1, n_groups, n_state, rp), lambda s, d, c: (s, d, 0, 0, 0))],
        out_shape=[jax.ShapeDtypeStruct((2, n_seq * seq, inner), BF16),
                   jax.ShapeDtypeStruct((n_seq, 2, n_groups, n_state, rp), F32)],
        scratch_shapes=[pltpu.VMEM((n_groups, n_state, rp), F32)],
        compiler_params=_params("parallel", "parallel", "arbitrary"),
        name="ssd_scan",
    )(xbc, dtr, pad_heads(a_log), pad_heads(dt_bias), dsk, tri, expand, h0)
    return y, hfin


def _gate_norm_kernel(yl_ref, yc_ref, z_ref, w_ref, o_ref, *, n_lat):
    def run(y_ref):
        v = (y_ref[0].astype(F32) + y_ref[1].astype(F32)) * _silu(z_ref[...].astype(F32))
        ms = jnp.mean(v * v, axis=-1, keepdims=True)
        o_ref[...] = (v * lax.rsqrt(ms + EPS) * w_ref[...]).astype(o_ref.dtype)

    is_lat = pl.program_id(0) < n_lat
    pl.when(is_lat)(lambda: run(yl_ref))
    pl.when(jnp.logical_not(is_lat))(lambda: run(yc_ref))


def _gate_norm(y_lat, y_ctx, p, z_off, w):
    inner = y_lat.shape[2]
    tm = ELT_TILE
    n_lat = y_lat.shape[1] // tm
    t = y_lat.shape[1] + y_ctx.shape[1]
    zb = z_off // inner
    yl_spec, yc_spec = _two_source_specs((2, tm, inner), n_lat, (0,))
    return pl.pallas_call(
        functools.partial(_gate_norm_kernel, n_lat=n_lat),
        grid=(t // tm,),
        in_specs=[yl_spec, yc_spec,
                  pl.BlockSpec((tm, inner), lambda i: (i, zb)),
                  pl.BlockSpec((1, inner), lambda i: (0, 0))],
        out_specs=pl.BlockSpec((tm, inner), lambda i: (i, 0)),
        out_shape=jax.ShapeDtypeStruct((t, inner), BF16),
        compiler_params=_params("parallel"),
        name="ssd_gate_norm",
    )(y_lat, y_ctx, p, w.reshape(1, inner))


def _pool_kernel(u_ref, band_ref, invc_ref, wp_ref, ps_ref, g_ref, prev_ref, o_ref):
    u = u_ref[...]
    win_sum = _dot(band_ref[0], u)
    m = win_sum * invc_ref[0][:, 0:1] - u.astype(F32)
    yb = _dot(m.astype(BF16), wp_ref[0]) * ps_ref[...]
    o_ref[...] = (prev_ref[...].astype(F32)
                  + jax.nn.sigmoid(g_ref[...].astype(F32)) * yb).astype(o_ref.dtype)


def _pool_tables(tp, row_len):
    nw = len(POOL_WINDOWS)
    band = np.zeros((nw, tp, tp), np.float32)
    invc = np.zeros((nw, tp, LANES), np.float32)
    j = np.arange(row_len)
    for k, w in enumerate(POOL_WINDOWS):
        lo = np.clip(j - w // 2, 0, row_len - 1)
        hi = np.clip(j + w // 2 - 1, 0, row_len - 1)
        for r0 in range(0, tp, row_len):
            for jj in range(row_len):
                band[k, r0 + jj, r0 + lo[jj]: r0 + hi[jj] + 1] = 1.0
                invc[k, r0 + jj, :] = 1.0 / (hi[jj] - lo[jj] + 1)
    return jnp.asarray(band, dtype=BF16), jnp.asarray(invc)


def _pool(p, merged, w_pool, l, pool_scale, u_off, gate_off, row0, n_rows, row_len):
    _, nw, pg, po = w_pool.shape
    d = merged.shape[1]
    tp = max(row_len, min(512, n_rows))
    band, invc = _pool_tables(tp, row_len)
    ub = u_off // pg
    gb = gate_off // po
    r0 = row0 // tp
    return pl.pallas_call(
        _pool_kernel,
        grid=(n_rows // tp, nw),
        in_specs=[pl.BlockSpec((tp, pg), lambda i, k: (r0 + i, ub + k)),
                  pl.BlockSpec((1, tp, tp), lambda i, k: (k, 0, 0)),
                  pl.BlockSpec((1, tp, LANES), lambda i, k: (k, 0, 0)),
                  pl.BlockSpec((None, 1, pg, po), lambda i, k: (l, k, 0, 0)),
                  pl.BlockSpec((1, po), lambda i, k: (0, k)),
                  pl.BlockSpec((tp, po), lambda i, k: (r0 + i, gb + k)),
                  pl.BlockSpec((tp, po), lambda i, k: (r0 + i, k))],
        out_specs=pl.BlockSpec((tp, po), lambda i, k: (r0 + i, k)),
        out_shape=jax.ShapeDtypeStruct(merged.shape, merged.dtype),
        input_output_aliases={6: 0},
        compiler_params=_params("parallel", "arbitrary"),
        name="pool_branch",
    )(p, band, invc, w_pool, pool_scale.reshape(1, d), p, merged)


def _confconv_kernel(a_ref, b_ref, w_ref, bias_ref, o_ref, scr, *, seq, taps, stride, chunk, pad):
    lanes = a_ref.shape[1]
    first = pad - (taps // 2) * stride
    scr[pl.ds(0, pad), :] = jnp.zeros((pad, lanes), F32)
    scr[pl.ds(pad + seq, pad), :] = jnp.zeros((pad, lanes), F32)
    scr[pl.ds(pad, seq), :] = a_ref[...].astype(F32) * jax.nn.sigmoid(b_ref[...].astype(F32))

    def do_chunk(base):
        acc = jnp.broadcast_to(bias_ref[...], (chunk, lanes))
        for k in range(taps):
            acc = acc + w_ref[k:k + 1, :] * scr[pl.ds(base + first + k * stride, chunk), :]
        o_ref[pl.ds(base, chunk), :] = acc

    n_chunks = seq // chunk
    if n_chunks == 1:
        do_chunk(0)
    else:
        def body(ci, carry):
            do_chunk(pl.multiple_of(ci * chunk, chunk))
            return carry
        lax.fori_loop(0, n_chunks, body, 0)


def _confconv(p, w, b, a_off, row0, seq, n_seq, stride):
    taps, cw = w.shape
    tc = LANES
    chunk = min(seq, 256)
    pad_alloc = -(-((taps // 2) * stride) // 8) * 8
    ab = a_off // tc
    bb = (a_off + cw) // tc
    r0 = row0 // seq
    kern = functools.partial(_confconv_kernel, seq=seq, taps=taps, stride=stride, chunk=chunk,
                             pad=pad_alloc)
    return pl.pallas_call(
        kern,
        grid=(n_seq, cw // tc),
        in_specs=[pl.BlockSpec((seq, tc), lambda s, j: (r0 + s, ab + j)),
                  pl.BlockSpec((seq, tc), lambda s, j: (r0 + s, bb + j)),
                  pl.BlockSpec((taps, tc), lambda s, j: (0, j)),
                  pl.BlockSpec((1, tc), lambda s, j: (0, j))],
        out_specs=pl.BlockSpec((seq, tc), lambda s, j: (s, j)),
        out_shape=jax.ShapeDtypeStruct((n_seq * seq, cw), F32),
        scratch_shapes=[pltpu.VMEM((seq + 2 * pad_alloc, tc), F32)],
        compiler_params=_params("parallel", "parallel"),
        name="conf_conv",
    )(p, p, w, b.reshape(1, cw))


def _ln_silu_kernel(xl_ref, xc_ref, w_ref, b_ref, o_ref, *, n_lat):
    def run(x_ref):
        x = x_ref[...]
        mu = jnp.mean(x, axis=-1, keepdims=True)
        xc = x - mu
        var = jnp.mean(xc * xc, axis=-1, keepdims=True)
        y = xc * lax.rsqrt(var + EPS) * w_ref[...] + b_ref[...]
        o_ref[...] = _silu(y).astype(o_ref.dtype)

    is_lat = pl.program_id(0) < n_lat
    pl.when(is_lat)(lambda: run(xl_ref))
    pl.when(jnp.logical_not(is_lat))(lambda: run(xc_ref))


def _ln_silu(v_lat, v_ctx, w, b):
    cw = v_lat.shape[1]
    tm = ELT_TILE
    n_lat = v_lat.shape[0] // tm
    t = v_lat.shape[0] + v_ctx.shape[0]
    xl_spec, xc_spec = _two_source_specs((tm, cw), n_lat, ())
    return pl.pallas_call(
        functools.partial(_ln_silu_kernel, n_lat=n_lat),
        grid=(t // tm,),
        in_specs=[xl_spec, xc_spec,
                  pl.BlockSpec((1, cw), lambda i: (0, 0)),
                  pl.BlockSpec((1, cw), lambda i: (0, 0))],
        out_specs=pl.BlockSpec((tm, cw), lambda i: (i, 0)),
        out_shape=jax.ShapeDtypeStruct((t, cw), BF16),
        compiler_params=_params("parallel"),
        name="conf_ln_silu",
    )(v_lat, v_ctx, w.reshape(1, cw), b.reshape(1, cw))


def _top1(rows):
    best, idx = rows[0], jnp.zeros(rows[0].shape, jnp.int32)
    for j in range(1, len(rows)):
        better = rows[j] > best
        idx = jnp.where(better, j, idx)
        best = jnp.where(better, rows[j], best)
    return best, idx


def _router_kernel(x_ref, w_ref, sc_ref, sh_ref, wr_ref, rb_ref, h_ref, sel_ref, *, n_experts):
    h2 = _prenorm_math(x_ref[...], w_ref[...], sc_ref[0], sh_ref[0])
    h_hi, h_lo = _split_bf16(h2, 2)
    h_ref[...] = _pack_bf16_pairs(h_hi.astype(F32))
    w_hi, w_lo = _split_bf16(wr_ref[...], 2)
    nt = (((1,), (1,)), ((), ()))

    def dg(a, b):
        return lax.dot_general(a, b, nt, preferred_element_type=F32)

    logits = dg(w_hi, h_hi) + dg(w_hi, h_lo) + dg(w_lo, h_hi)
    scores = jax.nn.sigmoid(logits)
    sel = scores + rb_ref[:, 0:1]
    per_group = n_experts // N_EXPERT_GROUPS
    sel_rows = [sel[e:e + 1, :] for e in range(n_experts)]
    score_rows = [scores[e:e + 1, :] for e in range(n_experts)]
    neg = jnp.full(sel_rows[0].shape, -jnp.inf, F32)

    group_scores = []
    for g in range(N_EXPERT_GROUPS):
        rows = sel_rows[g * per_group:(g + 1) * per_group]
        m1, i1 = _top1(rows)
        m2, _ = _top1([jnp.where(i1 == j, neg, rows[j]) for j in range(per_group)])
        group_scores.append(m1 + m2)
    _, best_group = _top1(group_scores)

    masked = [jnp.where(best_group == e // per_group, sel_rows[e], neg) for e in range(n_experts)]
    _, e1 = _top1(masked)
    _, e2 = _top1([jnp.where(e1 == e, neg, masked[e]) for e in range(n_experts)])
    s1 = sum(jnp.where(e1 == e, score_rows[e], 0.0) for e in range(n_experts))
    s2 = sum(jnp.where(e2 == e, score_rows[e], 0.0) for e in range(n_experts))
    tot = s1 + s2
    sel_ref[...] = jnp.zeros(sel_ref.shape, F32)
    sel_ref[0:1, :] = e1.astype(F32)
    sel_ref[1:2, :] = e2.astype(F32)
    sel_ref[2:3, :] = s1 / tot
    sel_ref[3:4, :] = s2 / tot


def _router(xa, w, mods, sc_col, sh_col, seq, n_batch, w_router, router_bias):
    t, d = xa.shape
    n_experts = w_router.shape[1]
    tm = ELT_TILE
    wr_t = w_router.astype(F32).T
    rb = jnp.broadcast_to(router_bias.astype(F32)[:, None], (n_experts, LANES))
    return pl.pallas_call(
        functools.partial(_router_kernel, n_experts=n_experts),
        grid=(t // tm,),
        in_specs=[pl.BlockSpec((tm, d), lambda i: (i, 0)),
                  pl.BlockSpec((1, d), lambda i: (0, 0)),
                  _mod_spec(d, sc_col, seq, n_batch + 1, tm),
                  _mod_spec(d, sh_col, seq, n_batch + 1, tm),
                  pl.BlockSpec((n_experts, d), lambda i: (0, 0)),
                  pl.BlockSpec((n_experts, LANES), lambda i: (0, 0))],
        out_specs=[pl.BlockSpec((tm, d // 2), lambda i: (i, 0)),
                   pl.BlockSpec((8, tm), lambda i: (0, i))],
        out_shape=[jax.ShapeDtypeStruct((t, d // 2), jnp.uint32),
                   jax.ShapeDtypeStruct((8, t), F32)],
        compiler_params=_params("parallel"),
        name="moe_prenorm_router",
    )(xa, w.reshape(1, d), mods, mods, wr_t, rb)


def _row_copy(src_hbm, row, dst, slot, sem):
    return pltpu.make_async_copy(src_hbm.at[pl.ds(row, 1), :], dst.at[pl.ds(slot, 1), :], sem)


def _gather_rows(src_hbm, idx_ref, dst, sem, n_rows, first=0):
    def issue(k, carry):
        r = first + k
        _row_copy(src_hbm, idx_ref[0, 0, r], dst, r, sem).start()
        return carry
    lax.fori_loop(0, n_rows, issue, 0, unroll=GATHER_UNROLL)


def _wait_rows(src_hbm, dst, sem, n_rows):
    pltpu.make_async_copy(src_hbm.at[pl.ds(0, n_rows), :], dst, sem).wait()


def _moe_kernel(te_ref, nv_ref, tok_ref, tok_next_ref, h_hbm, wg_ref, wu_ref, wd_ref, o_ref,
                xbuf, xb, hg, hu, sem, *, tm):
    i = pl.program_id(0)
    f = pl.program_id(1)
    last = pl.num_programs(1) - 1
    n_valid = nv_ref[0]
    valid = i < n_valid
    n_slices, _, kc = xb.shape
    per_half = n_slices // 2
    half = o_ref.shape[1]

    @pl.when(f == 0)
    def _():
        @pl.when(i == 0)
        def _():
            _gather_rows(h_hbm, tok_ref, xbuf, sem, tm)

        @pl.when(valid)
        def _():
            _wait_rows(h_hbm, xbuf, sem, tm)
            lo, hi = _unpack_bf16_pairs(xbuf[...])
            for s in range(n_slices):
                src = lo if s < per_half else hi
                off = (s % per_half) * kc
                xb[s] = src[:, off:off + kc].astype(BF16)

    @pl.when(i + 1 < n_valid)
    def _():
        share = tm // n_slices
        _gather_rows(h_hbm, tok_next_ref, xbuf, sem, share, first=f * share)

    @pl.when(valid)
    def _():
        x = xb[f]
        pg = _dot(x, wg_ref[...])
        pu = _dot(x, wu_ref[...])

        @pl.when(f == 0)
        def _():
            hg[...] = pg
            hu[...] = pu

        @pl.when(jnp.logical_and(f > 0, f < last))
        def _():
            hg[...] += pg
            hu[...] += pu

        @pl.when(f == last)
        def _():
            hid = (_silu(hg[...] + pg) * (hu[...] + pu)).astype(BF16)
            lo = _dot(hid, wd_ref[:, :half]).astype(BF16).astype(F32)
            hi = _dot(hid, wd_ref[:, half:]).astype(BF16).astype(F32)
            o_ref[...] = _pack_halves(lo, hi)

    @pl.when(jnp.logical_and(f == last, jnp.logical_not(valid)))
    def _():
        o_ref[...] = jnp.zeros(o_ref.shape, o_ref.dtype)


def _moe(h2p, tile_expert, n_valid, row_token, wg, wu, wd, l):
    half = h2p.shape[1]
    d = 2 * half
    ff = wg.shape[3]
    kc = d // MOE_K_SPLIT
    assert MOE_K_SPLIT % 2 == 0 and MOE_K_SPLIT >= 2
    tm = MOE_TILE
    n_tiles = tile_expert.shape[0]
    tok = row_token.reshape(n_tiles, 1, tm)
    w_in_spec = pl.BlockSpec((None, None, kc, ff), lambda i, f, te, nv: (l, te[i], f, 0))
    grid_spec = pltpu.PrefetchScalarGridSpec(
        num_scalar_prefetch=2,
        grid=(n_tiles, MOE_K_SPLIT),
        in_specs=[pl.BlockSpec((1, 1, tm), lambda i, f, te, nv: (i, 0, 0), memory_space=pltpu.SMEM),
                  pl.BlockSpec((1, 1, tm), lambda i, f, te, nv: (jnp.minimum(i + 1, n_tiles - 1), 0, 0),
                               memory_space=pltpu.SMEM),
                  pl.BlockSpec(memory_space=pl.ANY),
                  w_in_spec, w_in_spec,
                  pl.BlockSpec((None, None, ff, d), lambda i, f, te, nv: (l, te[i], 0, 0),
                               pipeline_mode=pl.Buffered(1))],
        out_specs=pl.BlockSpec((tm, half), lambda i, f, te, nv: (i, 0)),
        scratch_shapes=[pltpu.VMEM((tm, half), jnp.uint32), pltpu.VMEM((MOE_K_SPLIT, tm, kc), BF16),
                        pltpu.VMEM((tm, ff), F32), pltpu.VMEM((tm, ff), F32),
                        pltpu.SemaphoreType.DMA(())],
    )
    return pl.pallas_call(
        functools.partial(_moe_kernel, tm=tm),
        grid_spec=grid_spec,
        out_shape=jax.ShapeDtypeStruct((n_tiles * tm, half), jnp.uint32),
        compiler_params=_params("arbitrary", "arbitrary"),
        name="moe_experts",
    )(tile_expert, n_valid, tok, tok, h2p, wg, wu, wd)


def _combine_kernel(p0_ref, p1_ref, p0n_ref, p1n_ref, x_ref, gm_ref, gw_ref, nw_ref, sc_ref, sh_ref,
                    ys_hbm, *rest, tc, final):
    outs, (buf, sem) = rest[:-2], rest[-2:]
    i = pl.program_id(0)
    slot = lax.rem(i, 2)
    half = buf.shape[3]

    def gather(pa_ref, pb_ref, s):
        _gather_rows(ys_hbm, pa_ref, buf.at[s, 0], sem.at[s], tc)
        _gather_rows(ys_hbm, pb_ref, buf.at[s, 1], sem.at[s], tc)

    @pl.when(i == 0)
    def _():
        gather(p0_ref, p1_ref, 0)

    @pl.when(i + 1 < pl.num_programs(0))
    def _():
        gather(p0n_ref, p1n_ref, 1 - slot)

    _wait_rows(ys_hbm, buf.at[slot, 0], sem.at[slot], tc)
    _wait_rows(ys_hbm, buf.at[slot, 1], sem.at[slot], tc)
    lo0, hi0 = _unpack_bf16_pairs(buf[slot, 0])
    lo1, hi1 = _unpack_bf16_pairs(buf[slot, 1])
    w0 = gw_ref[:, 0:1]
    w1 = gw_ref[:, LANES:LANES + 1]
    g = gm_ref[0]
    new_lo = x_ref[:, :half] + g[:, :half] * (w0 * lo0 + w1 * lo1)
    new_hi = x_ref[:, half:] + g[:, half:] * (w0 * hi0 + w1 * hi1)
    ms = (jnp.sum(new_lo * new_lo, axis=-1, keepdims=True)
          + jnp.sum(new_hi * new_hi, axis=-1, keepdims=True)) * (1.0 / (2 * half))
    inv = lax.rsqrt(ms + EPS)
    nw = nw_ref[...]
    if final:
        o_ref, = outs
        o_ref[:, :half] = new_lo * inv * nw[:, :half]
        o_ref[:, half:] = new_hi * inv * nw[:, half:]
    else:
        x_out_ref, h_ref = outs
        x_out_ref[:, :half] = new_lo
        x_out_ref[:, half:] = new_hi
        sc = 1.0 + sc_ref[0]
        sh = sh_ref[0]
        h_ref[:, :half] = (new_lo * inv * nw[:, :half] * sc[:, :half] + sh[:, :half]).astype(h_ref.dtype)
        h_ref[:, half:] = (new_hi * inv * nw[:, half:] * sc[:, half:] + sh[:, half:]).astype(h_ref.dtype)


def _combine(xa, ys, pos0, pos1, gate_cols, mods, mod_col, seq, n_batch, norm_w, next_mods, final,
             n_rows):
    d = xa.shape[1]
    half = d // 2
    tc = ELT_TILE
    n = n_rows // tc
    p0 = pos0[:n_rows].reshape(n, 1, tc)
    p1 = pos1[:n_rows].reshape(n, 1, tc)
    cur = functools.partial(pl.BlockSpec, (1, 1, tc), lambda i: (i, 0, 0), memory_space=pltpu.SMEM)
    nxt = functools.partial(pl.BlockSpec, (1, 1, tc), lambda i: (jnp.minimum(i + 1, n - 1), 0, 0),
                            memory_space=pltpu.SMEM)
    row_spec = pl.BlockSpec((tc, d), lambda i: (i, 0))
    if final:
        out_specs = [row_spec]
        out_shape = [jax.ShapeDtypeStruct((n_rows, d), F32)]
        aliases = {}
    else:
        out_specs = [row_spec, row_spec]
        out_shape = [jax.ShapeDtypeStruct((n_rows, d), F32), jax.ShapeDtypeStruct((n_rows, d), BF16)]
        aliases = {4: 0}
    return pl.pallas_call(
        functools.partial(_combine_kernel, tc=tc, final=final),
        grid=(n,),
        in_specs=[cur(), cur(), nxt(), nxt(),
                  row_spec,
                  _mod_spec(d, mod_col, seq, n_batch + 1, tc),
                  pl.BlockSpec((tc, 2 * LANES), lambda i: (i, 0)),
                  pl.BlockSpec((1, d), lambda i: (0, 0)),
                  _mod_spec(d, 1, seq, n_batch + 1, tc),
                  _mod_spec(d, 0, seq, n_batch + 1, tc),
                  pl.BlockSpec(memory_space=pl.ANY)],
        out_specs=out_specs,
        out_shape=out_shape,
        scratch_shapes=[pltpu.VMEM((2, 2, tc, half), jnp.uint32), pltpu.SemaphoreType.DMA((2,))],
        input_output_aliases=aliases,
        compiler_params=_params("arbitrary"),
        name="moe_combine",
    )(p0, p1, p0, p1, xa, mods, gate_cols, norm_w.reshape(1, d), next_mods, next_mods, ys)


def _routing_tables(sel, n_experts, tm):
    t = sel.shape[1]
    e = sel[0:2].astype(jnp.int32).reshape(-1)
    onehot = (e[:, None] == jnp.arange(n_experts, dtype=jnp.int32)[None, :]).astype(jnp.int32)
    incl = jnp.cumsum(onehot, axis=0)
    rank = jnp.sum((incl - onehot) * onehot, axis=1)
    counts = incl[-1]
    tiles = (counts + tm - 1) // tm
    tile_end = jnp.cumsum(tiles)
    tile_start = tile_end - tiles
    pos = tile_start[e] * tm + rank
    n_tiles = (2 * t) // tm + n_experts
    token = jnp.tile(jnp.arange(t, dtype=jnp.int32), 2)
    row_token = jnp.zeros((n_tiles * tm,), jnp.int32).at[pos].set(token)
    n_valid = tile_end[-1]
    tile_ids = jnp.minimum(jnp.arange(n_tiles, dtype=jnp.int32), n_valid - 1)
    tile_expert = jnp.sum((tile_ids[:, None] >= tile_end[None, :]).astype(jnp.int32), axis=1)
    gate_cols = jnp.broadcast_to(sel[2:4].T[:, :, None], (t, 2, LANES)).reshape(t, 2 * LANES)
    return tile_expert, n_valid.reshape(1), row_token, pos[:t], pos[t:], gate_cols


def kernel(x, c, ctx, c_ctx, w_ada, b_ada, norm1_w, norm2_w, w_in, ssd_conv_w, ssd_conv_b,
           ssd_A_log, ssd_dt_bias, ssd_D, ssd_norm_w, w_ssd_out, w_pool, pool_scale,
           conf_conv_w, conf_conv_b, conf_ln_w, conf_ln_b, w_conf_out, w_out, w_router,
           router_bias, w_exp_gate, w_exp_up, w_exp_down, final_norm_w):
    n_batch, seq, d = x.shape
    ctx_len = ctx.shape[1]
    depth = w_in.shape[0]
    inner = w_ssd_out.shape[1]
    heads = ssd_A_log.shape[-1]
    xbc_w = ssd_conv_w.shape[-1]
    gn = (xbc_w - inner) // 2
    n_groups = gn // SSD_STATE
    hpg = heads // n_groups
    geo = (inner, n_groups, SSD_STATE, hpg, SSD_HEAD_DIM)
    pool_w = w_pool.shape[1] * w_pool.shape[2]
    conf_w = conf_conv_w.shape[-1]
    n_experts = w_router.shape[1]
    assert heads <= LANES and hpg * SSD_HEAD_DIM * n_groups == inner
    assert seq % GRID_W == 0 and seq % ROW_TILE == 0 and ctx_len % ELT_TILE == 0

    off_dt = xbc_w
    off_z = off_dt + 2 * heads
    off_pool = off_z + inner
    off_conf = off_pool + pool_w
    off_gate = off_conf + 2 * conf_w
    p_z = xbc_w
    p_pool = p_z + inner
    p_conf = p_pool + pool_w
    p_gate = p_conf + 2 * conf_w

    t_lat = n_batch * seq
    t_ctx = n_batch * ctx_len
    t_all = t_lat + t_ctx
    assert t_all % ROW_TILE == 0 and t_ctx % ROW_TILE == 0 and t_ctx <= seq
    x_rows = x.reshape(t_lat, d)
    ctx_rows = ctx.reshape(t_ctx, d)

    c_rows = jnp.zeros((BF16_SUBLANES, d), F32).at[:n_batch].set(c).at[n_batch].set(c_ctx)
    h0 = jnp.zeros((n_batch, 2, n_groups, SSD_STATE, hpg * SSD_HEAD_DIM), F32)

    w_dt = jnp.zeros((depth, d, 2 * LANES), F32)
    w_dt = w_dt.at[:, :, :heads].set(w_in[:, :, off_dt:off_dt + heads])
    w_dt = w_dt.at[:, :, LANES:LANES + heads].set(w_in[:, :, off_dt + heads:off_z]).astype(BF16)
    w_ssd_out_b = w_ssd_out.astype(BF16)
    w_pool_b = w_pool.astype(BF16)
    w_conf_out_b = w_conf_out.astype(BF16)
    w_out_b = w_out.astype(BF16)
    wg_b = w_exp_gate.astype(BF16)
    wu_b = w_exp_up.astype(BF16)
    wd_b = w_exp_down.astype(BF16)

    all_mods = [_ada(c_rows, w_ada, b_ada, l).reshape(BF16_SUBLANES, 1, 6 * d) for l in range(depth)]
    xa = None
    h = _prenorm(x_rows, ctx_rows, norm1_w[0], all_mods[0], 1, 0, seq, n_batch)
    for l in range(depth):
        mods = all_mods[l]

        p = _in_proj(h, w_in, l, off_dt, off_z - off_dt, p_gate + N_BRANCHES * d)
        dtr = _mm(h, w_dt, l, F32, name="dt_proj")

        xbc_ctx = _seqconv(p, ssd_conv_w[l], ssd_conv_b[l], t_lat, ctx_len, n_batch)
        xbc_lat = _seqconv(p, ssd_conv_w[l], ssd_conv_b[l], 0, seq, n_batch)
        y_ctx, s_ctx = _ssd(xbc_ctx, dtr, t_lat, ssd_A_log[l], ssd_dt_bias[l], ssd_D[l], h0,
                            ctx_len, n_batch, geo)
        y_lat, _ = _ssd(xbc_lat, dtr, 0, ssd_A_log[l], ssd_dt_bias[l], ssd_D[l], s_ctx,
                        seq, n_batch, geo)
        ya = _gate_norm(y_lat, y_ctx, p, p_z, ssd_norm_w[l])
        merged = _mm(ya, w_ssd_out_b, l, BF16, gate=p, gate_off=p_gate, name="ssd_out")

        merged = _pool(p, merged, w_pool_b, l, pool_scale[l], p_pool, p_gate + d, 0, t_lat, GRID_W)
        merged = _pool(p, merged, w_pool_b, l, pool_scale[l], p_pool, p_gate + d, t_lat, t_ctx,
                       ctx_len)

        v_lat = _confconv(p, conf_conv_w[l], conf_conv_b[l], p_conf, 0, seq, n_batch, GRID_W)
        v_ctx = _confconv(p, conf_conv_w[l], conf_conv_b[l], p_conf, t_lat, ctx_len, n_batch, 1)
        ca = _ln_silu(v_lat, v_ctx, conf_ln_w[l], conf_ln_b[l])
        merged = _mm(ca, w_conf_out_b, l, BF16, gate=p, gate_off=p_gate + 2 * d, prev=merged,
                     name="conf_out")

        resid = (x_rows, ctx_rows) if xa is None else xa
        xa = _mm(merged, w_out_b, l, F32, resid=resid, mods=mods, mod_col=2, seq=seq,
                 n_batch=n_batch, name="mixer_out")

        h2p, sel = _router(xa, norm2_w[l], mods, 4, 3, seq, n_batch, w_router, router_bias)
        tile_expert, n_valid, row_token, pos0, pos1, gate_cols = _routing_tables(
            sel, n_experts, MOE_TILE)
        ys = _moe(h2p, tile_expert, n_valid, row_token, wg_b, wu_b, wd_b, l)
        if l + 1 < depth:
            xa, h = _combine(xa, ys, pos0, pos1, gate_cols, mods, 5, seq, n_batch, norm1_w[l + 1],
                             all_mods[l + 1], False, t_all)
        else:
            out, = _combine(xa, ys, pos0, pos1, gate_cols, mods, 5, seq, n_batch, final_norm_w,
                            mods, True, t_lat)
    return out.reshape(n_batch, seq, d)
```

```python
import functools

import numpy as np
import jax
import jax.numpy as jnp
from jax import lax
from jax.experimental import pallas as pl
from jax.experimental.pallas import tpu as pltpu

GRID_W = 64
EPS = 1e-6
SSD_HEAD_DIM = 64
SSD_STATE = 128
SSD_CHUNK = 128
POOL_WINDOWS = (2, 4, 8, 16)
N_EXPERT_GROUPS = 4
N_BRANCHES = 3

LANES = 128
BF16_SUBLANES = 16
VMEM_LIMIT_BYTES = 56 * 1024 * 1024

ROW_TILE = 512
ELT_TILE = 256
MOE_TILE = 512
MOE_K_SPLIT = 4
GATHER_UNROLL = 8

F32 = jnp.float32
BF16 = jnp.bfloat16


def _params(*sem):
    return pltpu.CompilerParams(dimension_semantics=sem, vmem_limit_bytes=VMEM_LIMIT_BYTES)


def _pick_tile(n, *offsets, choices=(1024, 512, 256, 128)):
    for t in choices:
        if n % t == 0 and all(o % t == 0 for o in offsets):
            return t
    raise ValueError(f"no lane tile divides {n} and offsets {offsets}")


def _dot(a, b):
    return jnp.dot(a, b, preferred_element_type=F32)


def _split_bf16(v, parts):
    out = []
    rem = v
    for _ in range(parts):
        hi = rem.astype(BF16)
        out.append(hi)
        rem = rem - hi.astype(F32)
    return out


def _silu(v):
    return v * jax.nn.sigmoid(v)


_HIGH16 = np.uint32(0xFFFF0000)


def _pack_halves(lo, hi):
    return ((lax.bitcast_convert_type(lo, jnp.uint32) >> 16)
            | (lax.bitcast_convert_type(hi, jnp.uint32) & _HIGH16))


def _pack_bf16_pairs(v):
    half = v.shape[1] // 2
    return _pack_halves(v[:, :half], v[:, half:])


def _unpack_bf16_pairs(u):
    lo = lax.bitcast_convert_type(u << 16, F32)
    hi = lax.bitcast_convert_type(u & _HIGH16, F32)
    return lo, hi


def _ada_kernel(c_ref, w_ref, b_ref, o_ref):
    s = _silu(c_ref[...]).astype(BF16)
    o_ref[...] = _dot(s, w_ref[...].astype(BF16)) + b_ref[...]


def _ada(c_rows, w, b, l):
    m, d = c_rows.shape
    n = w.shape[2]
    tn = _pick_tile(n, choices=(512, 256, 128))
    return pl.pallas_call(
        _ada_kernel,
        grid=(n // tn,),
        in_specs=[pl.BlockSpec((m, d), lambda j: (0, 0)),
                  pl.BlockSpec((None, d, tn), lambda j: (l, 0, j)),
                  pl.BlockSpec((None, 1, tn), lambda j: (l, 0, j))],
        out_specs=pl.BlockSpec((m, tn), lambda j: (0, j)),
        out_shape=jax.ShapeDtypeStruct((m, n), F32),
        compiler_params=_params("arbitrary"),
        name="ada_mod",
    )(c_rows, w, b.reshape(b.shape[0], 1, n))


def _prenorm_math(x, w, sc, sh):
    ms = jnp.mean(x * x, axis=-1, keepdims=True)
    y = x * lax.rsqrt(ms + EPS) * w
    return y * (1.0 + sc) + sh


def _two_source_specs(block, n_lat, lead):
    def lat_map(i):
        return lead + (jnp.minimum(i, n_lat - 1), 0)

    def ctx_map(i):
        return lead + (jnp.maximum(i - n_lat, 0), 0)
    return pl.BlockSpec(block, lat_map), pl.BlockSpec(block, ctx_map)


def _prenorm_kernel(xl_ref, xc_ref, w_ref, sc_ref, sh_ref, o_ref, *, n_lat):
    def run(x_ref):
        o_ref[...] = _prenorm_math(x_ref[...], w_ref[...], sc_ref[0], sh_ref[0]).astype(o_ref.dtype)

    is_lat = pl.program_id(0) < n_lat
    pl.when(is_lat)(lambda: run(xl_ref))
    pl.when(jnp.logical_not(is_lat))(lambda: run(xc_ref))


def _mod_spec(d, col, rows_per_mod, n_mod, tm):
    tiles_per_mod = rows_per_mod // tm
    return pl.BlockSpec((1, 1, d), lambda i: (jnp.minimum(i // tiles_per_mod, n_mod - 1), 0, col))


def _prenorm(x_lat, x_ctx, w, mods, sc_col, sh_col, seq, n_batch):
    d = x_lat.shape[1]
    tm = ELT_TILE
    n_lat = x_lat.shape[0] // tm
    t = x_lat.shape[0] + x_ctx.shape[0]
    xl_spec, xc_spec = _two_source_specs((tm, d), n_lat, ())
    return pl.pallas_call(
        functools.partial(_prenorm_kernel, n_lat=n_lat),
        grid=(t // tm,),
        in_specs=[xl_spec, xc_spec,
                  pl.BlockSpec((1, d), lambda i: (0, 0)),
                  _mod_spec(d, sc_col, seq, n_batch + 1, tm),
                  _mod_spec(d, sh_col, seq, n_batch + 1, tm)],
        out_specs=pl.BlockSpec((tm, d), lambda i: (i, 0)),
        out_shape=jax.ShapeDtypeStruct((t, d), BF16),
        compiler_params=_params("parallel"),
        name="prenorm",
    )(x_lat, x_ctx, w.reshape(1, d), mods, mods)


def _mm_plain_kernel(a_ref, w_ref, o_ref):
    o_ref[...] = _dot(a_ref[...], w_ref[...]).astype(o_ref.dtype)


def _mm_gate_kernel(a_ref, w_ref, g_ref, o_ref):
    acc = _dot(a_ref[...], w_ref[...])
    o_ref[...] = (jax.nn.sigmoid(g_ref[...].astype(F32)) * acc).astype(o_ref.dtype)


def _mm_gate_acc_kernel(a_ref, w_ref, g_ref, prev_ref, o_ref):
    acc = _dot(a_ref[...], w_ref[...])
    o_ref[...] = (prev_ref[...].astype(F32)
                  + jax.nn.sigmoid(g_ref[...].astype(F32)) * acc).astype(o_ref.dtype)


def _mm_resid_kernel(a_ref, w_ref, x_ref, gm_ref, o_ref):
    acc = _dot(a_ref[...], w_ref[...])
    o_ref[...] = x_ref[...] + gm_ref[0] * acc


def _mm_resid2_kernel(a_ref, w_ref, xl_ref, xc_ref, gm_ref, o_ref, *, n_lat):
    acc = _dot(a_ref[...], w_ref[...])
    is_lat = pl.program_id(1) < n_lat

    @pl.when(is_lat)
    def _():
        o_ref[...] = xl_ref[...] + gm_ref[0] * acc

    @pl.when(jnp.logical_not(is_lat))
    def _():
        o_ref[...] = xc_ref[...] + gm_ref[0] * acc


def _mm(a, w, l, out_dtype, *, gate=None, gate_off=0, prev=None, resid=None, mods=None,
        mod_col=0, seq=0, n_batch=0, name="mm"):
    t, k = a.shape
    n = w.shape[2]
    tm = ROW_TILE
    tn = _pick_tile(n, gate_off)
    grid = (n // tn, t // tm)
    a_spec = pl.BlockSpec((tm, k), lambda j, i: (i, 0))
    w_spec = pl.BlockSpec((None, k, tn), lambda j, i: (l, 0, j))
    o_spec = pl.BlockSpec((tm, tn), lambda j, i: (i, j))
    goff = gate_off // tn
    g_spec = pl.BlockSpec((tm, tn), lambda j, i: (i, goff + j))
    aliases = {}
    if resid is not None:
        tiles_per_mod = seq // tm
        mcol = mod_col * (n // tn)
        gm_spec = pl.BlockSpec(
            (1, 1, tn), lambda j, i: (jnp.minimum(i // tiles_per_mod, n_batch), 0, mcol + j))
        if isinstance(resid, tuple):
            x_lat, x_ctx = resid
            n_lat = x_lat.shape[0] // tm
            xl_spec = pl.BlockSpec((tm, tn), lambda j, i: (jnp.minimum(i, n_lat - 1), j))
            xc_spec = pl.BlockSpec((tm, tn), lambda j, i: (jnp.maximum(i - n_lat, 0), j))
            kern = functools.partial(_mm_resid2_kernel, n_lat=n_lat)
            ins, specs = (a, w, x_lat, x_ctx, mods), [a_spec, w_spec, xl_spec, xc_spec, gm_spec]
        else:
            kern, ins, specs = _mm_resid_kernel, (a, w, resid, mods), [a_spec, w_spec, o_spec, gm_spec]
            aliases = {2: 0}
    elif prev is not None:
        kern, ins, specs = _mm_gate_acc_kernel, (a, w, gate, prev), [a_spec, w_spec, g_spec, o_spec]
        aliases = {3: 0}
    elif gate is not None:
        kern, ins, specs = _mm_gate_kernel, (a, w, gate), [a_spec, w_spec, g_spec]
    else:
        kern, ins, specs = _mm_plain_kernel, (a, w), [a_spec, w_spec]
    return pl.pallas_call(
        kern,
        grid=grid,
        in_specs=specs,
        out_specs=o_spec,
        out_shape=jax.ShapeDtypeStruct((t, n), out_dtype),
        input_output_aliases=aliases,
        compiler_params=_params("parallel", "arbitrary"),
        name=name,
    )(*ins)


def _in_proj_kernel(a_ref, wt_ref, wn_ref, o_ref, wb, *, n_plain, chunk_groups):
    j = pl.program_id(0)
    gpt, grp, k = wt_ref.shape
    rows = chunk_groups * grp

    def fill(skip):
        for c in range(0, gpt, chunk_groups):
            lo, hi = c + skip, c + skip + chunk_groups
            if hi <= gpt:
                blk = wt_ref[lo:hi]
            else:
                blk = jnp.concatenate([wt_ref[lo:gpt], wn_ref[0:hi - gpt]], axis=0)
            wb[:, pl.ds(c * grp, rows)] = blk.reshape(rows, k).astype(F32).T.astype(BF16)

    @pl.when(pl.program_id(1) == 0)
    def _():
        pl.when(j < n_plain)(lambda: fill(0))
        pl.when(j >= n_plain)(lambda: fill(1))

    o_ref[...] = _dot(a_ref[...], wb[...]).astype(o_ref.dtype)


def _in_proj(a, wt, l, n_plain_cols, skip_cols, n_out):
    t, k = a.shape
    tm = ROW_TILE
    tn = _pick_tile(n_out, n_plain_cols)
    grp = skip_cols
    assert wt.shape[1] == n_out + skip_cols and tn % grp == 0 and grp % BF16_SUBLANES == 0
    gpt = tn // grp
    chunk_groups = max(1, 256 // grp)
    assert gpt % chunk_groups == 0
    wt4 = wt.reshape(wt.shape[0], wt.shape[1] // grp, grp, k)
    kern = functools.partial(_in_proj_kernel, n_plain=n_plain_cols // tn, chunk_groups=chunk_groups)
    return pl.pallas_call(
        kern,
        grid=(n_out // tn, t // tm),
        in_specs=[pl.BlockSpec((tm, k), lambda j, i: (i, 0)),
                  pl.BlockSpec((None, gpt, grp, k), lambda j, i: (l, j, 0, 0)),
                  pl.BlockSpec((None, 1, grp, k), lambda j, i: (l, gpt * (j + 1), 0, 0))],
        out_specs=pl.BlockSpec((tm, tn), lambda j, i: (i, j)),
        out_shape=jax.ShapeDtypeStruct((t, n_out), BF16),
        scratch_shapes=[pltpu.VMEM((k, tn), BF16)],
        compiler_params=_params("arbitrary", "arbitrary"),
        name="in_proj",
    )(a, wt4, wt4)


def _seqconv_kernel(prev_ref, cur_ref, next_ref, w_ref, b_ref, o_ref, scr, *, ts, taps):
    i = pl.program_id(1)
    n = pl.num_programs(1)
    h = BF16_SUBLANES
    scr[pl.ds(0, h), :] = jnp.where(i > 0, prev_ref[...].astype(F32), 0.0)
    scr[pl.ds(h, ts), :] = cur_ref[...].astype(F32)
    scr[pl.ds(h + ts, h), :] = jnp.where(i < n - 1, next_ref[...].astype(F32), 0.0)
    acc = jnp.broadcast_to(b_ref[...], (ts, b_ref.shape[1]))
    for k in range(taps):
        acc = acc + w_ref[k:k + 1, :] * scr[pl.ds(h - taps // 2 + k, ts), :]
    o_ref[...] = _silu(acc).astype(o_ref.dtype)


def _seqconv(p, w, b, row0, seq, n_seq):
    taps, width = w.shape
    ts = min(seq, 512)
    tc = _pick_tile(width, choices=(512, 256, 128))
    h = BF16_SUBLANES
    tiles = seq // ts
    cur0 = row0 // ts
    halo0 = row0 // h
    per_tile = ts // h
    last_halo = (row0 + n_seq * seq) // h - 1

    def prev_map(s, i, j):
        return (jnp.maximum(halo0 + (s * tiles + i) * per_tile - 1, 0), j)

    def next_map(s, i, j):
        return (jnp.minimum(halo0 + (s * tiles + i + 1) * per_tile, last_halo), j)

    return pl.pallas_call(
        functools.partial(_seqconv_kernel, ts=ts, taps=taps),
        grid=(n_seq, tiles, width // tc),
        in_specs=[pl.BlockSpec((h, tc), prev_map),
                  pl.BlockSpec((ts, tc), lambda s, i, j: (cur0 + s * tiles + i, j)),
                  pl.BlockSpec((h, tc), next_map),
                  pl.BlockSpec((taps, tc), lambda s, i, j: (0, j)),
                  pl.BlockSpec((1, tc), lambda s, i, j: (0, j))],
        out_specs=pl.BlockSpec((ts, tc), lambda s, i, j: (s * tiles + i, j)),
        out_shape=jax.ShapeDtypeStruct((n_seq * seq, width), BF16),
        scratch_shapes=[pltpu.VMEM((ts + 2 * h, tc), F32)],
        compiler_params=_params("parallel", "parallel", "parallel"),
        name="ssd_seqconv",
    )(p, p, p, w, b.reshape(1, width))


def _ssd_kernel(xbc_ref, dt_ref, alog_ref, dtb_ref, dsk_ref, tri_ref, exp_ref, h0_ref,
                y_ref, hfin_ref, st, *, inner, n_groups, n_state, heads_per_group, head_dim):
    c = pl.program_id(2)
    q = dt_ref.shape[0]
    rp = heads_per_group * head_dim
    gn = n_groups * n_state

    @pl.when(c == 0)
    def _():
        st[...] = h0_ref[0, 0]

    x = dt_ref[...] + dtb_ref[0]
    dt = jnp.maximum(x, 0.0) + jnp.log1p(jnp.exp(-jnp.abs(x)))
    a = dt * (-jnp.exp(alog_ref[0]))
    tri = tri_ref[0]
    tri_b = tri.astype(BF16)
    acs = sum(_dot(tri_b, part) for part in _split_bf16(a, 3))
    atot = jnp.sum(a, axis=0, keepdims=True)
    acs_t = acs.T
    stack = jnp.concatenate(
        [dt, jnp.exp(acs), jnp.exp(atot - acs), jnp.broadcast_to(jnp.exp(atot), (8, LANES))], axis=0)
    expand = exp_ref[...]
    wide = sum(_dot(part, expand) for part in _split_bf16(stack, 2))
    dt_x = wide[0:q]
    eacs_x = wide[q:2 * q]
    edec_x = wide[2 * q:3 * q]
    etot_x = wide[3 * q:3 * q + 1]

    xs = xbc_ref[:, 0:inner].astype(F32)
    xdt = xs * dt_x
    xdt_b = xdt.astype(BF16)
    xw_b = (xdt * edec_x).astype(BF16)
    mask = tri > 0.0
    lane_head = lax.broadcasted_iota(jnp.int32, (1, rp), 1) // head_dim
    dsk = dsk_ref[0]

    for g in range(n_groups):
        bg = xbc_ref[:, inner + g * n_state: inner + (g + 1) * n_state]
        cg = xbc_ref[:, inner + gn + g * n_state: inner + gn + (g + 1) * n_state]
        bg_t = bg.astype(F32).T.astype(BF16)
        cb = _dot(cg, bg_t)
        sl = slice(g * rp, (g + 1) * rp)
        chunk_state = _dot(bg_t, xw_b[:, sl])
        st_g = st[g]
        y_off = _dot(cg, st_g.astype(BF16)) * eacs_x[:, sl]
        ms, xm = [], []
        for r in range(heads_per_group):
            hh = g * heads_per_group + r
            seg = acs[:, hh:hh + 1] - acs_t[hh:hh + 1, :]
            decay = jnp.exp(jnp.where(mask, seg, -jnp.inf))
            ms.append((cb * decay).astype(BF16))
            xm.append(jnp.where(lane_head == r, xdt_b[:, sl], jnp.zeros_like(xdt_b[:, sl])))
        y_diag = _dot(jnp.concatenate(ms, axis=1), jnp.concatenate(xm, axis=0))
        y = y_diag + y_off + dsk[:, sl] * xs[:, sl]
        y_ref[0, :, sl] = y.astype(y_ref.dtype)
        st[g] = st_g * etot_x[:, sl] + chunk_state

    @pl.when(c == pl.num_programs(2) - 1)
    def _():
        hfin_ref[0, 0] = st[...]


def _ssd(xbc, dtr, dt_row0, a_log, dt_bias, d_skip, h0, seq, n_seq, geo):
    inner, n_groups, n_state, hpg, head_dim = geo
    heads = n_groups * hpg
    q = SSD_CHUNK
    nc = seq // q
    rp = hpg * head_dim
    xbc_w = xbc.shape[1]

    def pad_heads(v):
        return jnp.pad(v.astype(F32), ((0, 0), (0, LANES - heads))).reshape(2, 1, LANES)

    dsk = jnp.repeat(d_skip.astype(F32), head_dim, axis=-1).reshape(2, 1, inner)
    low = np.tril(np.ones((q, q), np.float32))
    tri = jnp.asarray(np.stack([low, low.T]))
    expand = np.zeros((LANES, inner), np.float32)
    for hh in range(heads):
        expand[hh, hh * head_dim:(hh + 1) * head_dim] = 1.0
    expand = jnp.asarray(expand, dtype=BF16)
    dt_blk0 = dt_row0 // q

    def chunk(s, d, c):
        return s * nc + c + d * (nc - 1 - 2 * c)

    kern = functools.partial(_ssd_kernel, inner=inner, n_groups=n_groups, n_state=n_state,
                             heads_per_group=hpg, head_dim=head_dim)
    y, hfin = pl.pallas_call(
        kern,
        grid=(n_seq, 2, nc),
        in_specs=[pl.BlockSpec((q, xbc_w), lambda s, d, c: (chunk(s, d, c), 0)),
                  pl.BlockSpec((q, LANES), lambda s, d, c: (dt_blk0 + chunk(s, d, c), d)),
                  pl.BlockSpec((1, 1, LANES), lambda s, d, c: (d, 0, 0)),
                  pl.BlockSpec((1, 1, LANES), lambda s, d, c: (d, 0, 0)),
                  pl.BlockSpec((1, 1, inner), lambda s, d, c: (d, 0, 0)),
                  pl.BlockSpec((1, q, q), lambda s, d, c: (d, 0, 0)),
                  pl.BlockSpec((LANES, inner), lambda s, d, c: (0, 0)),
                  pl.BlockSpec((1, 1, n_groups, n_state, rp), lambda s, d, c: (s, d, 0, 0, 0))],
        out_specs=[pl.BlockSpec((1, q, inner), lambda s, d, c: (d, chunk(s, d, c), 0)),
                   pl.BlockSpec((1, 1, n_groups, n_state, rp), lambda s, d, c: (s, d, 0, 0, 0))],
        out_shape=[jax.ShapeDtypeStruct((2, n_seq * seq, inner), BF16),
                   jax.ShapeDtypeStruct((n_seq, 2, n_groups, n_state, rp), F32)],
        scratch_shapes=[pltpu.VMEM((n_groups, n_state, rp), F32)],
        compiler_params=_params("parallel", "parallel", "arbitrary"),
        name="ssd_scan",
    )(xbc, dtr, pad_heads(a_log), pad_heads(dt_bias), dsk, tri, expand, h0)
    return y, hfin


def _gate_norm_kernel(yl_ref, yc_ref, z_ref, w_ref, o_ref, *, n_lat):
    def run(y_ref):
        v = (y_ref[0].astype(F32) + y_ref[1].astype(F32)) * _silu(z_ref[...].astype(F32))
        ms = jnp.mean(v * v, axis=-1, keepdims=True)
        o_ref[...] = (v * lax.rsqrt(ms + EPS) * w_ref[...]).astype(o_ref.dtype)

    is_lat = pl.program_id(0) < n_lat
    pl.when(is_lat)(lambda: run(yl_ref))
    pl.when(jnp.logical_not(is_lat))(lambda: run(yc_ref))


def _gate_norm(y_lat, y_ctx, p, z_off, w):
    inner = y_lat.shape[2]
    tm = ELT_TILE
    n_lat = y_lat.shape[1] // tm
    t = y_lat.shape[1] + y_ctx.shape[1]
    zb = z_off // inner
    yl_spec, yc_spec = _two_source_specs((2, tm, inner), n_lat, (0,))
    return pl.pallas_call(
        functools.partial(_gate_norm_kernel, n_lat=n_lat),
        grid=(t // tm,),
        in_specs=[yl_spec, yc_spec,
                  pl.BlockSpec((tm, inner), lambda i: (i, zb)),
                  pl.BlockSpec((1, inner), lambda i: (0, 0))],
        out_specs=pl.BlockSpec((tm, inner), lambda i: (i, 0)),
        out_shape=jax.ShapeDtypeStruct((t, inner), BF16),
        compiler_params=_params("parallel"),
        name="ssd_gate_norm",
    )(y_lat, y_ctx, p, w.reshape(1, inner))


def _pool_kernel(u_ref, band_ref, invc_ref, wp_ref, ps_ref, g_ref, prev_ref, o_ref):
    u = u_ref[...]
    win_sum = _dot(band_ref[0], u)
    m = win_sum * invc_ref[0][:, 0:1] - u.astype(F32)
    yb = _dot(m.astype(BF16), wp_ref[0]) * ps_ref[...]
    o_ref[...] = (prev_ref[...].astype(F32)
                  + jax.nn.sigmoid(g_ref[...].astype(F32)) * yb).astype(o_ref.dtype)


def _pool_tables(tp, row_len):
    nw = len(POOL_WINDOWS)
    band = np.zeros((nw, tp, tp), np.float32)
    invc = np.zeros((nw, tp, LANES), np.float32)
    j = np.arange(row_len)
    for k, w in enumerate(POOL_WINDOWS):
        lo = np.clip(j - w // 2, 0, row_len - 1)
        hi = np.clip(j + w // 2 - 1, 0, row_len - 1)
        for r0 in range(0, tp, row_len):
            for jj in range(row_len):
                band[k, r0 + jj, r0 + lo[jj]: r0 + hi[jj] + 1] = 1.0
                invc[k, r0 + jj, :] = 1.0 / (hi[jj] - lo[jj] + 1)
    return jnp.asarray(band, dtype=BF16), jnp.asarray(invc)


def _pool(p, merged, w_pool, l, pool_scale, u_off, gate_off, row0, n_rows, row_len):
    _, nw, pg, po = w_pool.shape
    d = merged.shape[1]
    tp = max(row_len, min(512, n_rows))
    band, invc = _pool_tables(tp, row_len)
    ub = u_off // pg
    gb = gate_off // po
    r0 = row0 // tp
    return pl.pallas_call(
        _pool_kernel,
        grid=(n_rows // tp, nw),
        in_specs=[pl.BlockSpec((tp, pg), lambda i, k: (r0 + i, ub + k)),
                  pl.BlockSpec((1, tp, tp), lambda i, k: (k, 0, 0)),
                  pl.BlockSpec((1, tp, LANES), lambda i, k: (k, 0, 0)),
                  pl.BlockSpec((None, 1, pg, po), lambda i, k: (l, k, 0, 0)),
                  pl.BlockSpec((1, po), lambda i, k: (0, k)),
                  pl.BlockSpec((tp, po), lambda i, k: (r0 + i, gb + k)),
                  pl.BlockSpec((tp, po), lambda i, k: (r0 + i, k))],
        out_specs=pl.BlockSpec((tp, po), lambda i, k: (r0 + i, k)),
        out_shape=jax.ShapeDtypeStruct(merged.shape, merged.dtype),
        input_output_aliases={6: 0},
        compiler_params=_params("parallel", "arbitrary"),
        name="pool_branch",
    )(p, band, invc, w_pool, pool_scale.reshape(1, d), p, merged)


def _confconv_kernel(a_ref, b_ref, w_ref, bias_ref, o_ref, scr, *, seq, taps, stride, chunk, pad):
    lanes = a_ref.shape[1]
    first = pad - (taps // 2) * stride
    scr[pl.ds(0, pad), :] = jnp.zeros((pad, lanes), F32)
    scr[pl.ds(pad + seq, pad), :] = jnp.zeros((pad, lanes), F32)
    scr[pl.ds(pad, seq), :] = a_ref[...].astype(F32) * jax.nn.sigmoid(b_ref[...].astype(F32))

    def do_chunk(base):
        acc = jnp.broadcast_to(bias_ref[...], (chunk, lanes))
        for k in range(taps):
            acc = acc + w_ref[k:k + 1, :] * scr[pl.ds(base + first + k * stride, chunk), :]
        o_ref[pl.ds(base, chunk), :] = acc

    n_chunks = seq // chunk
    if n_chunks == 1:
        do_chunk(0)
    else:
        def body(ci, carry):
            do_chunk(pl.multiple_of(ci * chunk, chunk))
            return carry
        lax.fori_loop(0, n_chunks, body, 0)


def _confconv(p, w, b, a_off, row0, seq, n_seq, stride):
    taps, cw = w.shape
    tc = LANES
    chunk = min(seq, 256)
    pad_alloc = -(-((taps // 2) * stride) // 8) * 8
    ab = a_off // tc
    bb = (a_off + cw) // tc
    r0 = row0 // seq
    kern = functools.partial(_confconv_kernel, seq=seq, taps=taps, stride=stride, chunk=chunk,
                             pad=pad_alloc)
    return pl.pallas_call(
        kern,
        grid=(n_seq, cw // tc),
        in_specs=[pl.BlockSpec((seq, tc), lambda s, j: (r0 + s, ab + j)),
                  pl.BlockSpec((seq, tc), lambda s, j: (r0 + s, bb + j)),
                  pl.BlockSpec((taps, tc), lambda s, j: (0, j)),
                  pl.BlockSpec((1, tc), lambda s, j: (0, j))],
        out_specs=pl.BlockSpec((seq, tc), lambda s, j: (s, j)),
        out_shape=jax.ShapeDtypeStruct((n_seq * seq, cw), F32),
        scratch_shapes=[pltpu.VMEM((seq + 2 * pad_alloc, tc), F32)],
        compiler_params=_params("parallel", "parallel"),
        name="conf_conv",
    )(p, p, w, b.reshape(1, cw))


def _ln_silu_kernel(xl_ref, xc_ref, w_ref, b_ref, o_ref, *, n_lat):
    def run(x_ref):
        x = x_ref[...]
        mu = jnp.mean(x, axis=-1, keepdims=True)
        xc = x - mu
        var = jnp.mean(xc * xc, axis=-1, keepdims=True)
        y = xc * lax.rsqrt(var + EPS) * w_ref[...] + b_ref[...]
        o_ref[...] = _silu(y).astype(o_ref.dtype)

    is_lat = pl.program_id(0) < n_lat
    pl.when(is_lat)(lambda: run(xl_ref))
    pl.when(jnp.logical_not(is_lat))(lambda: run(xc_ref))


def _ln_silu(v_lat, v_ctx, w, b):
    cw = v_lat.shape[1]
    tm = ELT_TILE
    n_lat = v_lat.shape[0] // tm
    t = v_lat.shape[0] + v_ctx.shape[0]
    xl_spec, xc_spec = _two_source_specs((tm, cw), n_lat, ())
    return pl.pallas_call(
        functools.partial(_ln_silu_kernel, n_lat=n_lat),
        grid=(t // tm,),
        in_specs=[xl_spec, xc_spec,
                  pl.BlockSpec((1, cw), lambda i: (0, 0)),
                  pl.BlockSpec((1, cw), lambda i: (0, 0))],
        out_specs=pl.BlockSpec((tm, cw), lambda i: (i, 0)),
        out_shape=jax.ShapeDtypeStruct((t, cw), BF16),
        compiler_params=_params("parallel"),
        name="conf_ln_silu",
    )(v_lat, v_ctx, w.reshape(1, cw), b.reshape(1, cw))


def _top1(rows):
    best, idx = rows[0], jnp.zeros(rows[0].shape, jnp.int32)
    for j in range(1, len(rows)):
        better = rows[j] > best
        idx = jnp.where(better, j, idx)
        best = jnp.where(better, rows[j], best)
    return best, idx


def _router_kernel(x_ref, w_ref, sc_ref, sh_ref, wr_ref, rb_ref, h_ref, sel_ref, *, n_experts):
    h2 = _prenorm_math(x_ref[...], w_ref[...], sc_ref[0], sh_ref[0])
    h_hi, h_lo = _split_bf16(h2, 2)
    h_ref[...] = _pack_bf16_pairs(h_hi.astype(F32))
    w_hi, w_lo = _split_bf16(wr_ref[...], 2)
    nt = (((1,), (1,)), ((), ()))

    def dg(a, b):
        return lax.dot_general(a, b, nt, preferred_element_type=F32)

    logits = dg(w_hi, h_hi) + dg(w_hi, h_lo) + dg(w_lo, h_hi)
    scores = jax.nn.sigmoid(logits)
    sel = scores + rb_ref[:, 0:1]
    per_group = n_experts // N_EXPERT_GROUPS
    sel_rows = [sel[e:e + 1, :] for e in range(n_experts)]
    score_rows = [scores[e:e + 1, :] for e in range(n_experts)]
    neg = jnp.full(sel_rows[0].shape, -jnp.inf, F32)

    group_scores = []
    for g in range(N_EXPERT_GROUPS):
        rows = sel_rows[g * per_group:(g + 1) * per_group]
        m1, i1 = _top1(rows)
        m2, _ = _top1([jnp.where(i1 == j, neg, rows[j]) for j in range(per_group)])
        group_scores.append(m1 + m2)
    _, best_group = _top1(group_scores)

    masked = [jnp.where(best_group == e // per_group, sel_rows[e], neg) for e in range(n_experts)]
    _, e1 = _top1(masked)
    _, e2 = _top1([jnp.where(e1 == e, neg, masked[e]) for e in range(n_experts)])
    s1 = sum(jnp.where(e1 == e, score_rows[e], 0.0) for e in range(n_experts))
    s2 = sum(jnp.where(e2 == e, score_rows[e], 0.0) for e in range(n_experts))
    tot = s1 + s2
    sel_ref[...] = jnp.zeros(sel_ref.shape, F32)
    sel_ref[0:1, :] = e1.astype(F32)
    sel_ref[1:2, :] = e2.astype(F32)
    sel_ref[2:3, :] = s1 / tot
    sel_ref[3:4, :] = s2 / tot


def _router(xa, w, mods, sc_col, sh_col, seq, n_batch, w_router, router_bias):
    t, d = xa.shape
    n_experts = w_router.shape[1]
    tm = ELT_TILE
    wr_t = w_router.astype(F32).T
    rb = jnp.broadcast_to(router_bias.astype(F32)[:, None], (n_experts, LANES))
    return pl.pallas_call(
        functools.partial(_router_kernel, n_experts=n_experts),
        grid=(t // tm,),
        in_specs=[pl.BlockSpec((tm, d), lambda i: (i, 0)),
                  pl.BlockSpec((1, d), lambda i: (0, 0)),
                  _mod_spec(d, sc_col, seq, n_batch + 1, tm),
                  _mod_spec(d, sh_col, seq, n_batch + 1, tm),
                  pl.BlockSpec((n_experts, d), lambda i: (0, 0)),
                  pl.BlockSpec((n_experts, LANES), lambda i: (0, 0))],
        out_specs=[pl.BlockSpec((tm, d // 2), lambda i: (i, 0)),
                   pl.BlockSpec((8, tm), lambda i: (0, i))],
        out_shape=[jax.ShapeDtypeStruct((t, d // 2), jnp.uint32),
                   jax.ShapeDtypeStruct((8, t), F32)],
        compiler_params=_params("parallel"),
        name="moe_prenorm_router",
    )(xa, w.reshape(1, d), mods, mods, wr_t, rb)


def _row_copy(src_hbm, row, dst, slot, sem):
    return pltpu.make_async_copy(src_hbm.at[pl.ds(row, 1), :], dst.at[pl.ds(slot, 1), :], sem)


def _gather_rows(src_hbm, idx_ref, dst, sem, n_rows, first=0):
    def issue(k, carry):
        r = first + k
        _row_copy(src_hbm, idx_ref[0, 0, r], dst, r, sem).start()
        return carry
    lax.fori_loop(0, n_rows, issue, 0, unroll=GATHER_UNROLL)


def _wait_rows(src_hbm, dst, sem, n_rows):
    pltpu.make_async_copy(src_hbm.at[pl.ds(0, n_rows), :], dst, sem).wait()


def _moe_kernel(te_ref, nv_ref, tok_ref, tok_next_ref, h_hbm, wg_ref, wu_ref, wd_ref, o_ref,
                xbuf, xb, hg, hu, sem, *, tm):
    i = pl.program_id(0)
    f = pl.program_id(1)
    last = pl.num_programs(1) - 1
    n_valid = nv_ref[0]
    valid = i < n_valid
    n_slices, _, kc = xb.shape
    per_half = n_slices // 2
    half = o_ref.shape[1]

    @pl.when(f == 0)
    def _():
        @pl.when(i == 0)
        def _():
            _gather_rows(h_hbm, tok_ref, xbuf, sem, tm)

        @pl.when(valid)
        def _():
            _wait_rows(h_hbm, xbuf, sem, tm)
            lo, hi = _unpack_bf16_pairs(xbuf[...])
            for s in range(n_slices):
                src = lo if s < per_half else hi
                off = (s % per_half) * kc
                xb[s] = src[:, off:off + kc].astype(BF16)

    @pl.when(i + 1 < n_valid)
    def _():
        share = tm // n_slices
        _gather_rows(h_hbm, tok_next_ref, xbuf, sem, share, first=f * share)

    @pl.when(valid)
    def _():
        x = xb[f]
        pg = _dot(x, wg_ref[...])
        pu = _dot(x, wu_ref[...])

        @pl.when(f == 0)
        def _():
            hg[...] = pg
            hu[...] = pu

        @pl.when(jnp.logical_and(f > 0, f < last))
        def _():
            hg[...] += pg
            hu[...] += pu

        @pl.when(f == last)
        def _():
            hid = (_silu(hg[...] + pg) * (hu[...] + pu)).astype(BF16)
            lo = _dot(hid, wd_ref[:, :half]).astype(BF16).astype(F32)
            hi = _dot(hid, wd_ref[:, half:]).astype(BF16).astype(F32)
            o_ref[...] = _pack_halves(lo, hi)

    @pl.when(jnp.logical_and(f == last, jnp.logical_not(valid)))
    def _():
        o_ref[...] = jnp.zeros(o_ref.shape, o_ref.dtype)


def _moe(h2p, tile_expert, n_valid, row_token, wg, wu, wd, l):
    half = h2p.shape[1]
    d = 2 * half
    ff = wg.shape[3]
    kc = d // MOE_K_SPLIT
    assert MOE_K_SPLIT % 2 == 0 and MOE_K_SPLIT >= 2
    tm = MOE_TILE
    n_tiles = tile_expert.shape[0]
    tok = row_token.reshape(n_tiles, 1, tm)
    w_in_spec = pl.BlockSpec((None, None, kc, ff), lambda i, f, te, nv: (l, te[i], f, 0))
    grid_spec = pltpu.PrefetchScalarGridSpec(
        num_scalar_prefetch=2,
        grid=(n_tiles, MOE_K_SPLIT),
        in_specs=[pl.BlockSpec((1, 1, tm), lambda i, f, te, nv: (i, 0, 0), memory_space=pltpu.SMEM),
                  pl.BlockSpec((1, 1, tm), lambda i, f, te, nv: (jnp.minimum(i + 1, n_tiles - 1), 0, 0),
                               memory_space=pltpu.SMEM),
                  pl.BlockSpec(memory_space=pl.ANY),
                  w_in_spec, w_in_spec,
                  pl.BlockSpec((None, None, ff, d), lambda i, f, te, nv: (l, te[i], 0, 0),
                               pipeline_mode=pl.Buffered(1))],
        out_specs=pl.BlockSpec((tm, half), lambda i, f, te, nv: (i, 0)),
        scratch_shapes=[pltpu.VMEM((tm, half), jnp.uint32), pltpu.VMEM((MOE_K_SPLIT, tm, kc), BF16),
                        pltpu.VMEM((tm, ff), F32), pltpu.VMEM((tm, ff), F32),
                        pltpu.SemaphoreType.DMA(())],
    )
    return pl.pallas_call(
        functools.partial(_moe_kernel, tm=tm),
        grid_spec=grid_spec,
        out_shape=jax.ShapeDtypeStruct((n_tiles * tm, half), jnp.uint32),
        compiler_params=_params("arbitrary", "arbitrary"),
        name="moe_experts",
    )(tile_expert, n_valid, tok, tok, h2p, wg, wu, wd)


def _combine_kernel(p0_ref, p1_ref, p0n_ref, p1n_ref, x_ref, gm_ref, gw_ref, nw_ref, sc_ref, sh_ref,
                    ys_hbm, *rest, tc, final):
    outs, (buf, sem) = rest[:-2], rest[-2:]
    i = pl.program_id(0)
    slot = lax.rem(i, 2)
    half = buf.shape[3]

    def gather(pa_ref, pb_ref, s):
        _gather_rows(ys_hbm, pa_ref, buf.at[s, 0], sem.at[s], tc)
        _gather_rows(ys_hbm, pb_ref, buf.at[s, 1], sem.at[s], tc)

    @pl.when(i == 0)
    def _():
        gather(p0_ref, p1_ref, 0)

    @pl.when(i + 1 < pl.num_programs(0))
    def _():
        gather(p0n_ref, p1n_ref, 1 - slot)

    _wait_rows(ys_hbm, buf.at[slot, 0], sem.at[slot], tc)
    _wait_rows(ys_hbm, buf.at[slot, 1], sem.at[slot], tc)
    lo0, hi0 = _unpack_bf16_pairs(buf[slot, 0])
    lo1, hi1 = _unpack_bf16_pairs(buf[slot, 1])
    w0 = gw_ref[:, 0:1]
    w1 = gw_ref[:, LANES:LANES + 1]
    g = gm_ref[0]
    new_lo = x_ref[:, :half] + g[:, :half] * (w0 * lo0 + w1 * lo1)
    new_hi = x_ref[:, half:] + g[:, half:] * (w0 * hi0 + w1 * hi1)
    ms = (jnp.sum(new_lo * new_lo, axis=-1, keepdims=True)
          + jnp.sum(new_hi * new_hi, axis=-1, keepdims=True)) * (1.0 / (2 * half))
    inv = lax.rsqrt(ms + EPS)
    nw = nw_ref[...]
    if final:
        o_ref, = outs
        o_ref[:, :half] = new_lo * inv * nw[:, :half]
        o_ref[:, half:] = new_hi * inv * nw[:, half:]
    else:
        x_out_ref, h_ref = outs
        x_out_ref[:, :half] = new_lo
        x_out_ref[:, half:] = new_hi
        sc = 1.0 + sc_ref[0]
        sh = sh_ref[0]
        h_ref[:, :half] = (new_lo * inv * nw[:, :half] * sc[:, :half] + sh[:, :half]).astype(h_ref.dtype)
        h_ref[:, half:] = (new_hi * inv * nw[:, half:] * sc[:, half:] + sh[:, half:]).astype(h_ref.dtype)


def _combine(xa, ys, pos0, pos1, gate_cols, mods, mod_col, seq, n_batch, norm_w, next_mods, final,
             n_rows):
    d = xa.shape[1]
    half = d // 2
    tc = ELT_TILE
    n = n_rows // tc
    p0 = pos0[:n_rows].reshape(n, 1, tc)
    p1 = pos1[:n_rows].reshape(n, 1, tc)
    cur = functools.partial(pl.BlockSpec, (1, 1, tc), lambda i: (i, 0, 0), memory_space=pltpu.SMEM)
    nxt = functools.partial(pl.BlockSpec, (1, 1, tc), lambda i: (jnp.minimum(i + 1, n - 1), 0, 0),
                            memory_space=pltpu.SMEM)
    row_spec = pl.BlockSpec((tc, d), lambda i: (i, 0))
    if final:
        out_specs = [row_spec]
        out_shape = [jax.ShapeDtypeStruct((n_rows, d), F32)]
        aliases = {}
    else:
        out_specs = [row_spec, row_spec]
        out_shape = [jax.ShapeDtypeStruct((n_rows, d), F32), jax.ShapeDtypeStruct((n_rows, d), BF16)]
        aliases = {4: 0}
    return pl.pallas_call(
        functools.partial(_combine_kernel, tc=tc, final=final),
        grid=(n,),
        in_specs=[cur(), cur(), nxt(), nxt(),
                  row_spec,
                  _mod_spec(d, mod_col, seq, n_batch + 1, tc),
                  pl.BlockSpec((tc, 2 * LANES), lambda i: (i, 0)),
                  pl.BlockSpec((1, d), lambda i: (0, 0)),
                  _mod_spec(d, 1, seq, n_batch + 1, tc),
                  _mod_spec(d, 0, seq, n_batch + 1, tc),
                  pl.BlockSpec(memory_space=pl.ANY)],
        out_specs=out_specs,
        out_shape=out_shape,
        scratch_shapes=[pltpu.VMEM((2, 2, tc, half), jnp.uint32), pltpu.SemaphoreType.DMA((2,))],
        input_output_aliases=aliases,
        compiler_params=_params("arbitrary"),
        name="moe_combine",
    )(p0, p1, p0, p1, xa, mods, gate_cols, norm_w.reshape(1, d), next_mods, next_mods, ys)


def _routing_tables(sel, n_experts, tm):
    t = sel.shape[1]
    e = sel[0:2].astype(jnp.int32).reshape(-1)
    onehot = (e[:, None] == jnp.arange(n_experts, dtype=jnp.int32)[None, :]).astype(jnp.int32)
    incl = jnp.cumsum(onehot, axis=0)
    rank = jnp.sum((incl - onehot) * onehot, axis=1)
    counts = incl[-1]
    tiles = (counts + tm - 1) // tm
    tile_end = jnp.cumsum(tiles)
    tile_start = tile_end - tiles
    pos = tile_start[e] * tm + rank
    n_tiles = (2 * t) // tm + n_experts
    token = jnp.tile(jnp.arange(t, dtype=jnp.int32), 2)
    row_token = jnp.zeros((n_tiles * tm,), jnp.int32).at[pos].set(token)
    n_valid = tile_end[-1]
    tile_ids = jnp.minimum(jnp.arange(n_tiles, dtype=jnp.int32), n_valid - 1)
    tile_expert = jnp.sum((tile_ids[:, None] >= tile_end[None, :]).astype(jnp.int32), axis=1)
    gate_cols = jnp.broadcast_to(sel[2:4].T[:, :, None], (t, 2, LANES)).reshape(t, 2 * LANES)
    return tile_expert, n_valid.reshape(1), row_token, pos[:t], pos[t:], gate_cols


def kernel(x, c, ctx, c_ctx, w_ada, b_ada, norm1_w, norm2_w, w_in, ssd_conv_w, ssd_conv_b,
           ssd_A_log, ssd_dt_bias, ssd_D, ssd_norm_w, w_ssd_out, w_pool, pool_scale,
           conf_conv_w, conf_conv_b, conf_ln_w, conf_ln_b, w_conf_out, w_out, w_router,
           router_bias, w_exp_gate, w_exp_up, w_exp_down, final_norm_w):
    n_batch, seq, d = x.shape
    ctx_len = ctx.shape[1]
    depth = w_in.shape[0]
    inner = w_ssd_out.shape[1]
    heads = ssd_A_log.shape[-1]
    xbc_w = ssd_conv_w.shape[-1]
    gn = (xbc_w - inner) // 2
    n_groups = gn // SSD_STATE
    hpg = heads // n_groups
    geo = (inner, n_groups, SSD_STATE, hpg, SSD_HEAD_DIM)
    pool_w = w_pool.shape[1] * w_pool.shape[2]
    conf_w = conf_conv_w.shape[-1]
    n_experts = w_router.shape[1]
    assert heads <= LANES and hpg * SSD_HEAD_DIM * n_groups == inner
    assert seq % GRID_W == 0 and seq % ROW_TILE == 0 and ctx_len % ELT_TILE == 0

    off_dt = xbc_w
    off_z = off_dt + 2 * heads
    off_pool = off_z + inner
    off_conf = off_pool + pool_w
    off_gate = off_conf + 2 * conf_w
    p_z = xbc_w
    p_pool = p_z + inner
    p_conf = p_pool + pool_w
    p_gate = p_conf + 2 * conf_w

    t_lat = n_batch * seq
    t_ctx = n_batch * ctx_len
    t_all = t_lat + t_ctx
    assert t_all % ROW_TILE == 0 and t_ctx % ROW_TILE == 0 and t_ctx <= seq
    x_rows = x.reshape(t_lat, d)
    ctx_rows = ctx.reshape(t_ctx, d)

    c_rows = jnp.zeros((BF16_SUBLANES, d), F32).at[:n_batch].set(c).at[n_batch].set(c_ctx)
    h0 = jnp.zeros((n_batch, 2, n_groups, SSD_STATE, hpg * SSD_HEAD_DIM), F32)

    wt_in = jnp.swapaxes(w_in, 1, 2).astype(BF16)
    w_dt = jnp.zeros((depth, d, 2 * LANES), F32)
    w_dt = w_dt.at[:, :, :heads].set(w_in[:, :, off_dt:off_dt + heads])
    w_dt = w_dt.at[:, :, LANES:LANES + heads].set(w_in[:, :, off_dt + heads:off_z]).astype(BF16)
    w_ssd_out_b = w_ssd_out.astype(BF16)
    w_pool_b = w_pool.astype(BF16)
    w_conf_out_b = w_conf_out.astype(BF16)
    w_out_b = w_out.astype(BF16)
    wg_b = w_exp_gate.astype(BF16)
    wu_b = w_exp_up.astype(BF16)
    wd_b = w_exp_down.astype(BF16)

    all_mods = [_ada(c_rows, w_ada, b_ada, l).reshape(BF16_SUBLANES, 1, 6 * d) for l in range(depth)]
    xa = None
    h = _prenorm(x_rows, ctx_rows, norm1_w[0], all_mods[0], 1, 0, seq, n_batch)
    for l in range(depth):
        mods = all_mods[l]

        p = _in_proj(h, wt_in, l, off_dt, off_z - off_dt, p_gate + N_BRANCHES * d)
        dtr = _mm(h, w_dt, l, F32, name="dt_proj")

        xbc_ctx = _seqconv(p, ssd_conv_w[l], ssd_conv_b[l], t_lat, ctx_len, n_batch)
        xbc_lat = _seqconv(p, ssd_conv_w[l], ssd_conv_b[l], 0, seq, n_batch)
        y_ctx, s_ctx = _ssd(xbc_ctx, dtr, t_lat, ssd_A_log[l], ssd_dt_bias[l], ssd_D[l], h0,
                            ctx_len, n_batch, geo)
        y_lat, _ = _ssd(xbc_lat, dtr, 0, ssd_A_log[l], ssd_dt_bias[l], ssd_D[l], s_ctx,
                        seq, n_batch, geo)
        ya = _gate_norm(y_lat, y_ctx, p, p_z, ssd_norm_w[l])
        merged = _mm(ya, w_ssd_out_b, l, BF16, gate=p, gate_off=p_gate, name="ssd_out")

        merged = _pool(p, merged, w_pool_b, l, pool_scale[l], p_pool, p_gate + d, 0, t_lat, GRID_W)
        merged = _pool(p, merged, w_pool_b, l, pool_scale[l], p_pool, p_gate + d, t_lat, t_ctx,
                       ctx_len)

        v_lat = _confconv(p, conf_conv_w[l], conf_conv_b[l], p_conf, 0, seq, n_batch, GRID_W)
        v_ctx = _confconv(p, conf_conv_w[l], conf_conv_b[l], p_conf, t_lat, ctx_len, n_batch, 1)
        ca = _ln_silu(v_lat, v_ctx, conf_ln_w[l], conf_ln_b[l])
        merged = _mm(ca, w_conf_out_b, l, BF16, gate=p, gate_off=p_gate + 2 * d, prev=merged,
                     name="conf_out")

        resid = (x_rows, ctx_rows) if xa is None else xa
        xa = _mm(merged, w_out_b, l, F32, resid=resid, mods=mods, mod_col=2, seq=seq,
                 n_batch=n_batch, name="mixer_out")

        h2p, sel = _router(xa, norm2_w[l], mods, 4, 3, seq, n_batch, w_router, router_bias)
        tile_expert, n_valid, row_token, pos0, pos1, gate_cols = _routing_tables(
            sel, n_experts, MOE_TILE)
        ys = _moe(h2p, tile_expert, n_valid, row_token, wg_b, wu_b, wd_b, l)
        if l + 1 < depth:
            xa, h = _combine(xa, ys, pos0, pos1, gate_cols, mods, 5, seq, n_batch, norm1_w[l + 1],
                             all_mods[l + 1], False, t_all)
        else:
            out, = _combine(xa, ys, pos0, pos1, gate_cols, mods, 5, seq, n_batch, final_norm_w,
                            mods, True, t_lat)
    return out.reshape(n_batch, seq, d)
```

```python
import functools

import numpy as np
import jax
import jax.numpy as jnp
from jax import lax
from jax.experimental import pallas as pl
from jax.experimental.pallas import tpu as pltpu

GRID_W = 64
EPS = 1e-6
SSD_HEAD_DIM = 64
SSD_STATE = 128
SSD_CHUNK = 128
POOL_WINDOWS = (2, 4, 8, 16)
N_EXPERT_GROUPS = 4
N_BRANCHES = 3

LANES = 128
BF16_SUBLANES = 16
VMEM_LIMIT_BYTES = 56 * 1024 * 1024

ROW_TILE = 512
ELT_TILE = 256
MOE_TILE = 512
MOE_K_SPLIT = 4
GATHER_UNROLL = 8

F32 = jnp.float32
BF16 = jnp.bfloat16


def _params(*sem):
    return pltpu.CompilerParams(dimension_semantics=sem, vmem_limit_bytes=VMEM_LIMIT_BYTES)


def _pick_tile(n, *offsets, choices=(1024, 512, 256, 128)):
    for t in choices:
        if n % t == 0 and all(o % t == 0 for o in offsets):
            return t
    raise ValueError(f"no lane tile divides {n} and offsets {offsets}")


def _dot(a, b):
    return jnp.dot(a, b, preferred_element_type=F32)


def _split_bf16(v, parts):
    out = []
    rem = v
    for _ in range(parts):
        hi = rem.astype(BF16)
        out.append(hi)
        rem = rem - hi.astype(F32)
    return out


def _silu(v):
    return v * jax.nn.sigmoid(v)


_HIGH16 = np.uint32(0xFFFF0000)


def _pack_halves(lo, hi):
    return ((lax.bitcast_convert_type(lo, jnp.uint32) >> 16)
            | (lax.bitcast_convert_type(hi, jnp.uint32) & _HIGH16))


def _pack_bf16_pairs(v):
    half = v.shape[1] // 2
    return _pack_halves(v[:, :half], v[:, half:])


def _unpack_bf16_pairs(u):
    lo = lax.bitcast_convert_type(u << 16, F32)
    hi = lax.bitcast_convert_type(u & _HIGH16, F32)
    return lo, hi


def _ada_kernel(c_ref, w_ref, b_ref, o_ref):
    s = _silu(c_ref[...]).astype(BF16)
    o_ref[...] = _dot(s, w_ref[...].astype(BF16)) + b_ref[...]


def _ada(c_rows, w, b, l):
    m, d = c_rows.shape
    n = w.shape[2]
    tn = _pick_tile(n, choices=(512, 256, 128))
    return pl.pallas_call(
        _ada_kernel,
        grid=(n // tn,),
        in_specs=[pl.BlockSpec((m, d), lambda j: (0, 0)),
                  pl.BlockSpec((None, d, tn), lambda j: (l, 0, j)),
                  pl.BlockSpec((None, 1, tn), lambda j: (l, 0, j))],
        out_specs=pl.BlockSpec((m, tn), lambda j: (0, j)),
        out_shape=jax.ShapeDtypeStruct((m, n), F32),
        compiler_params=_params("arbitrary"),
        name="ada_mod",
    )(c_rows, w, b.reshape(b.shape[0], 1, n))


def _prenorm_math(x, w, sc, sh):
    ms = jnp.mean(x * x, axis=-1, keepdims=True)
    y = x * lax.rsqrt(ms + EPS) * w
    return y * (1.0 + sc) + sh


def _two_source_specs(block, n_lat, lead):
    def lat_map(i):
        return lead + (jnp.minimum(i, n_lat - 1), 0)

    def ctx_map(i):
        return lead + (jnp.maximum(i - n_lat, 0), 0)
    return pl.BlockSpec(block, lat_map), pl.BlockSpec(block, ctx_map)


def _prenorm_kernel(xl_ref, xc_ref, w_ref, sc_ref, sh_ref, o_ref, *, n_lat):
    def run(x_ref):
        o_ref[...] = _prenorm_math(x_ref[...], w_ref[...], sc_ref[0], sh_ref[0]).astype(o_ref.dtype)

    is_lat = pl.program_id(0) < n_lat
    pl.when(is_lat)(lambda: run(xl_ref))
    pl.when(jnp.logical_not(is_lat))(lambda: run(xc_ref))


def _mod_spec(d, col, rows_per_mod, n_mod, tm):
    tiles_per_mod = rows_per_mod // tm
    return pl.BlockSpec((1, 1, d), lambda i: (jnp.minimum(i // tiles_per_mod, n_mod - 1), 0, col))


def _prenorm(x_lat, x_ctx, w, mods, sc_col, sh_col, seq, n_batch):
    d = x_lat.shape[1]
    tm = ELT_TILE
    n_lat = x_lat.shape[0] // tm
    t = x_lat.shape[0] + x_ctx.shape[0]
    xl_spec, xc_spec = _two_source_specs((tm, d), n_lat, ())
    return pl.pallas_call(
        functools.partial(_prenorm_kernel, n_lat=n_lat),
        grid=(t // tm,),
        in_specs=[xl_spec, xc_spec,
                  pl.BlockSpec((1, d), lambda i: (0, 0)),
                  _mod_spec(d, sc_col, seq, n_batch + 1, tm),
                  _mod_spec(d, sh_col, seq, n_batch + 1, tm)],
        out_specs=pl.BlockSpec((tm, d), lambda i: (i, 0)),
        out_shape=jax.ShapeDtypeStruct((t, d), BF16),
        compiler_params=_params("parallel"),
        name="prenorm",
    )(x_lat, x_ctx, w.reshape(1, d), mods, mods)


def _mm_plain_kernel(a_ref, w_ref, o_ref):
    o_ref[...] = _dot(a_ref[...], w_ref[...]).astype(o_ref.dtype)


def _mm_gate_kernel(a_ref, w_ref, g_ref, o_ref):
    acc = _dot(a_ref[...], w_ref[...])
    o_ref[...] = (jax.nn.sigmoid(g_ref[...].astype(F32)) * acc).astype(o_ref.dtype)


def _mm_gate_acc_kernel(a_ref, w_ref, g_ref, prev_ref, o_ref):
    acc = _dot(a_ref[...], w_ref[...])
    o_ref[...] = (prev_ref[...].astype(F32)
                  + jax.nn.sigmoid(g_ref[...].astype(F32)) * acc).astype(o_ref.dtype)


def _mm_resid_kernel(a_ref, w_ref, x_ref, gm_ref, o_ref):
    acc = _dot(a_ref[...], w_ref[...])
    o_ref[...] = x_ref[...] + gm_ref[0] * acc


def _mm_resid2_kernel(a_ref, w_ref, xl_ref, xc_ref, gm_ref, o_ref, *, n_lat):
    acc = _dot(a_ref[...], w_ref[...])
    is_lat = pl.program_id(1) < n_lat

    @pl.when(is_lat)
    def _():
        o_ref[...] = xl_ref[...] + gm_ref[0] * acc

    @pl.when(jnp.logical_not(is_lat))
    def _():
        o_ref[...] = xc_ref[...] + gm_ref[0] * acc


def _mm(a, w, l, out_dtype, *, gate=None, gate_off=0, prev=None, resid=None, mods=None,
        mod_col=0, seq=0, n_batch=0, name="mm"):
    t, k = a.shape
    n = w.shape[2]
    tm = ROW_TILE
    tn = _pick_tile(n, gate_off)
    grid = (n // tn, t // tm)
    a_spec = pl.BlockSpec((tm, k), lambda j, i: (i, 0))
    w_spec = pl.BlockSpec((None, k, tn), lambda j, i: (l, 0, j))
    o_spec = pl.BlockSpec((tm, tn), lambda j, i: (i, j))
    goff = gate_off // tn
    g_spec = pl.BlockSpec((tm, tn), lambda j, i: (i, goff + j))
    aliases = {}
    if resid is not None:
        tiles_per_mod = seq // tm
        mcol = mod_col * (n // tn)
        gm_spec = pl.BlockSpec(
            (1, 1, tn), lambda j, i: (jnp.minimum(i // tiles_per_mod, n_batch), 0, mcol + j))
        if isinstance(resid, tuple):
            x_lat, x_ctx = resid
            n_lat = x_lat.shape[0] // tm
            xl_spec = pl.BlockSpec((tm, tn), lambda j, i: (jnp.minimum(i, n_lat - 1), j))
            xc_spec = pl.BlockSpec((tm, tn), lambda j, i: (jnp.maximum(i - n_lat, 0), j))
            kern = functools.partial(_mm_resid2_kernel, n_lat=n_lat)
            ins, specs = (a, w, x_lat, x_ctx, mods), [a_spec, w_spec, xl_spec, xc_spec, gm_spec]
        else:
            kern, ins, specs = _mm_resid_kernel, (a, w, resid, mods), [a_spec, w_spec, o_spec, gm_spec]
            aliases = {2: 0}
    elif prev is not None:
        kern, ins, specs = _mm_gate_acc_kernel, (a, w, gate, prev), [a_spec, w_spec, g_spec, o_spec]
        aliases = {3: 0}
    elif gate is not None:
        kern, ins, specs = _mm_gate_kernel, (a, w, gate), [a_spec, w_spec, g_spec]
    else:
        kern, ins, specs = _mm_plain_kernel, (a, w), [a_spec, w_spec]
    return pl.pallas_call(
        kern,
        grid=grid,
        in_specs=specs,
        out_specs=o_spec,
        out_shape=jax.ShapeDtypeStruct((t, n), out_dtype),
        input_output_aliases=aliases,
        compiler_params=_params("parallel", "arbitrary"),
        name=name,
    )(*ins)


def _in_proj_kernel(a_ref, wt_ref, wn_ref, o_ref, wb, *, n_plain, chunk_groups):
    j = pl.program_id(0)
    gpt, grp, k = wt_ref.shape
    rows = chunk_groups * grp

    def fill(skip):
        for c in range(0, gpt, chunk_groups):
            lo, hi = c + skip, c + skip + chunk_groups
            if hi <= gpt:
                blk = wt_ref[lo:hi]
            else:
                blk = jnp.concatenate([wt_ref[lo:gpt], wn_ref[0:hi - gpt]], axis=0)
            wb[:, pl.ds(c * grp, rows)] = blk.reshape(rows, k).astype(F32).T.astype(BF16)

    @pl.when(pl.program_id(1) == 0)
    def _():
        pl.when(j < n_plain)(lambda: fill(0))
        pl.when(j >= n_plain)(lambda: fill(1))

    o_ref[...] = _dot(a_ref[...], wb[...]).astype(o_ref.dtype)


def _in_proj(a, wt, l, n_plain_cols, skip_cols, n_out):
    t, k = a.shape
    tm = ROW_TILE
    tn = _pick_tile(n_out, n_plain_cols)
    grp = skip_cols
    assert wt.shape[1] == n_out + skip_cols and tn % grp == 0 and grp % BF16_SUBLANES == 0
    gpt = tn // grp
    chunk_groups = max(1, 256 // grp)
    assert gpt % chunk_groups == 0
    wt4 = wt.reshape(wt.shape[0], wt.shape[1] // grp, grp, k)
    kern = functools.partial(_in_proj_kernel, n_plain=n_plain_cols // tn, chunk_groups=chunk_groups)
    return pl.pallas_call(
        kern,
        grid=(n_out // tn, t // tm),
        in_specs=[pl.BlockSpec((tm, k), lambda j, i: (i, 0)),
                  pl.BlockSpec((None, gpt, grp, k), lambda j, i: (l, j, 0, 0)),
                  pl.BlockSpec((None, 1, grp, k), lambda j, i: (l, gpt * (j + 1), 0, 0))],
        out_specs=pl.BlockSpec((tm, tn), lambda j, i: (i, j)),
        out_shape=jax.ShapeDtypeStruct((t, n_out), BF16),
        scratch_shapes=[pltpu.VMEM((k, tn), BF16)],
        compiler_params=_params("arbitrary", "arbitrary"),
        name="in_proj",
    )(a, wt4, wt4)


def _seqconv_kernel(prev_ref, cur_ref, next_ref, w_ref, b_ref, o_ref, scr, *, ts, taps):
    i = pl.program_id(1)
    n = pl.num_programs(1)
    h = BF16_SUBLANES
    scr[pl.ds(0, h), :] = jnp.where(i > 0, prev_ref[...].astype(F32), 0.0)
    scr[pl.ds(h, ts), :] = cur_ref[...].astype(F32)
    scr[pl.ds(h + ts, h), :] = jnp.where(i < n - 1, next_ref[...].astype(F32), 0.0)
    acc = jnp.broadcast_to(b_ref[...], (ts, b_ref.shape[1]))
    for k in range(taps):
        acc = acc + w_ref[k:k + 1, :] * scr[pl.ds(h - taps // 2 + k, ts), :]
    o_ref[...] = _silu(acc).astype(o_ref.dtype)


def _seqconv(p, w, b, row0, seq, n_seq):
    taps, width = w.shape
    ts = min(seq, 512)
    tc = _pick_tile(width, choices=(512, 256, 128))
    h = BF16_SUBLANES
    tiles = seq // ts
    cur0 = row0 // ts
    halo0 = row0 // h
    per_tile = ts // h
    last_halo = (row0 + n_seq * seq) // h - 1

    def prev_map(s, i, j):
        return (jnp.maximum(halo0 + (s * tiles + i) * per_tile - 1, 0), j)

    def next_map(s, i, j):
        return (jnp.minimum(halo0 + (s * tiles + i + 1) * per_tile, last_halo), j)

    return pl.pallas_call(
        functools.partial(_seqconv_kernel, ts=ts, taps=taps),
        grid=(n_seq, tiles, width // tc),
        in_specs=[pl.BlockSpec((h, tc), prev_map),
                  pl.BlockSpec((ts, tc), lambda s, i, j: (cur0 + s * tiles + i, j)),
                  pl.BlockSpec((h, tc), next_map),
                  pl.BlockSpec((taps, tc), lambda s, i, j: (0, j)),
                  pl.BlockSpec((1, tc), lambda s, i, j: (0, j))],
        out_specs=pl.BlockSpec((ts, tc), lambda s, i, j: (s * tiles + i, j)),
        out_shape=jax.ShapeDtypeStruct((n_seq * seq, width), BF16),
        scratch_shapes=[pltpu.VMEM((ts + 2 * h, tc), F32)],
        compiler_params=_params("parallel", "parallel", "parallel"),
        name="ssd_seqconv",
    )(p, p, p, w, b.reshape(1, width))


def _ssd_kernel(xbc_ref, dt_ref, alog_ref, dtb_ref, dsk_ref, tri_ref, exp_ref, h0_ref,
                y_ref, hfin_ref, st, *, inner, n_groups, n_state, heads_per_group, head_dim):
    c = pl.program_id(2)
    q = dt_ref.shape[0]
    rp = heads_per_group * head_dim
    gn = n_groups * n_state

    @pl.when(c == 0)
    def _():
        st[...] = h0_ref[0, 0]

    x = dt_ref[...] + dtb_ref[0]
    dt = jnp.maximum(x, 0.0) + jnp.log1p(jnp.exp(-jnp.abs(x)))
    a = dt * (-jnp.exp(alog_ref[0]))
    tri = tri_ref[0]
    tri_b = tri.astype(BF16)
    acs = sum(_dot(tri_b, part) for part in _split_bf16(a, 3))
    atot = jnp.sum(a, axis=0, keepdims=True)
    acs_t = acs.T
    stack = jnp.concatenate(
        [dt, jnp.exp(acs), jnp.exp(atot - acs), jnp.broadcast_to(jnp.exp(atot), (8, LANES))], axis=0)
    expand = exp_ref[...]
    wide = sum(_dot(part, expand) for part in _split_bf16(stack, 2))
    dt_x = wide[0:q]
    eacs_x = wide[q:2 * q]
    edec_x = wide[2 * q:3 * q]
    etot_x = wide[3 * q:3 * q + 1]

    xs = xbc_ref[:, 0:inner].astype(F32)
    xdt = xs * dt_x
    xdt_b = xdt.astype(BF16)
    xw_b = (xdt * edec_x).astype(BF16)
    mask = tri > 0.0
    lane_head = lax.broadcasted_iota(jnp.int32, (1, rp), 1) // head_dim
    dsk = dsk_ref[0]

    for g in range(n_groups):
        bg = xbc_ref[:, inner + g * n_state: inner + (g + 1) * n_state]
        cg = xbc_ref[:, inner + gn + g * n_state: inner + gn + (g + 1) * n_state]
        bg_t = bg.astype(F32).T.astype(BF16)
        cb = _dot(cg, bg_t)
        sl = slice(g * rp, (g + 1) * rp)
        chunk_state = _dot(bg_t, xw_b[:, sl])
        st_g = st[g]
        y_off = _dot(cg, st_g.astype(BF16)) * eacs_x[:, sl]
        ms, xm = [], []
        for r in range(heads_per_group):
            hh = g * heads_per_group + r
            seg = acs[:, hh:hh + 1] - acs_t[hh:hh + 1, :]
            decay = jnp.exp(jnp.where(mask, seg, -jnp.inf))
            ms.append((cb * decay).astype(BF16))
            xm.append(jnp.where(lane_head == r, xdt_b[:, sl], jnp.zeros_like(xdt_b[:, sl])))
        y_diag = _dot(jnp.concatenate(ms, axis=1), jnp.concatenate(xm, axis=0))
        y = y_diag + y_off + dsk[:, sl] * xs[:, sl]
        y_ref[0, :, sl] = y.astype(y_ref.dtype)
        st[g] = st_g * etot_x[:, sl] + chunk_state

    @pl.when(c == pl.num_programs(2) - 1)
    def _():
        hfin_ref[0, 0] = st[...]


def _ssd(xbc, dtr, dt_row0, a_log, dt_bias, d_skip, h0, seq, n_seq, geo):
    inner, n_groups, n_state, hpg, head_dim = geo
    heads = n_groups * hpg
    q = SSD_CHUNK
    nc = seq // q
    rp = hpg * head_dim
    xbc_w = xbc.shape[1]

    def pad_heads(v):
        return jnp.pad(v.astype(F32), ((0, 0), (0, LANES - heads))).reshape(2, 1, LANES)

    dsk = jnp.repeat(d_skip.astype(F32), head_dim, axis=-1).reshape(2, 1, inner)
    low = np.tril(np.ones((q, q), np.float32))
    tri = jnp.asarray(np.stack([low, low.T]))
    expand = np.zeros((LANES, inner), np.float32)
    for hh in range(heads):
        expand[hh, hh * head_dim:(hh + 1) * head_dim] = 1.0
    expand = jnp.asarray(expand, dtype=BF16)
    dt_blk0 = dt_row0 // q

    def chunk(s, d, c):
        return s * nc + c + d * (nc - 1 - 2 * c)

    kern = functools.partial(_ssd_kernel, inner=inner, n_groups=n_groups, n_state=n_state,
                             heads_per_group=hpg, head_dim=head_dim)
    y, hfin = pl.pallas_call(
        kern,
        grid=(n_seq, 2, nc),
        in_specs=[pl.BlockSpec((q, xbc_w), lambda s, d, c: (chunk(s, d, c), 0)),
                  pl.BlockSpec((q, LANES), lambda s, d, c: (dt_blk0 + chunk(s, d, c), d)),
                  pl.BlockSpec((1, 1, LANES), lambda s, d, c: (d, 0, 0)),
                  pl.BlockSpec((1, 1, LANES), lambda s, d, c: (d, 0, 0)),
                  pl.BlockSpec((1, 1, inner), lambda s, d, c: (d, 0, 0)),
                  pl.BlockSpec((1, q, q), lambda s, d, c: (d, 0, 0)),
                  pl.BlockSpec((LANES, inner), lambda s, d, c: (0, 0)),
                  pl.BlockSpec((1, 1, n_groups, n_state, rp), lambda s, d, c: (s, d, 0, 0, 0))],
        out_specs=[pl.BlockSpec((1, q, inner), lambda s, d, c: (d, chunk(s, d, c), 0)),
                   pl.BlockSpec((1, 1, n_groups, n_state, rp), lambda s, d, c: (s, d, 0, 0, 0))],
        out_shape=[jax.ShapeDtypeStruct((2, n_seq * seq, inner), BF16),
                   jax.ShapeDtypeStruct((n_seq, 2, n_groups, n_state, rp), F32)],
        scratch_shapes=[pltpu.VMEM((n_groups, n_state, rp), F32)],
        compiler_params=_params("parallel", "parallel", "arbitrary"),
        name="ssd_scan",
    )(xbc, dtr, pad_heads(a_log), pad_heads(dt_bias), dsk, tri, expand, h0)
    return y, hfin


def _gate_norm_kernel(yl_ref, yc_ref, z_ref, w_ref, o_ref, *, n_lat):
    def run(y_ref):
        v = (y_ref[0].astype(F32) + y_ref[1].astype(F32)) * _silu(z_ref[...].astype(F32))
        ms = jnp.mean(v * v, axis=-1, keepdims=True)
        o_ref[...] = (v * lax.rsqrt(ms + EPS) * w_ref[...]).astype(o_ref.dtype)

    is_lat = pl.program_id(0) < n_lat
    pl.when(is_lat)(lambda: run(yl_ref))
    pl.when(jnp.logical_not(is_lat))(lambda: run(yc_ref))


def _gate_norm(y_lat, y_ctx, p, z_off, w):
    inner = y_lat.shape[2]
    tm = ELT_TILE
    n_lat = y_lat.shape[1] // tm
    t = y_lat.shape[1] + y_ctx.shape[1]
    zb = z_off // inner
    yl_spec, yc_spec = _two_source_specs((2, tm, inner), n_lat, (0,))
    return pl.pallas_call(
        functools.partial(_gate_norm_kernel, n_lat=n_lat),
        grid=(t // tm,),
        in_specs=[yl_spec, yc_spec,
                  pl.BlockSpec((tm, inner), lambda i: (i, zb)),
                  pl.BlockSpec((1, inner), lambda i: (0, 0))],
        out_specs=pl.BlockSpec((tm, inner), lambda i: (i, 0)),
        out_shape=jax.ShapeDtypeStruct((t, inner), BF16),
        compiler_params=_params("parallel"),
        name="ssd_gate_norm",
    )(y_lat, y_ctx, p, w.reshape(1, inner))


def _pool_kernel(u_ref, band_ref, invc_ref, wp_ref, ps_ref, g_ref, prev_ref, o_ref):
    u = u_ref[...]
    win_sum = _dot(band_ref[0], u)
    m = win_sum * invc_ref[0][:, 0:1] - u.astype(F32)
    yb = _dot(m.astype(BF16), wp_ref[0]) * ps_ref[...]
    o_ref[...] = (prev_ref[...].astype(F32)
                  + jax.nn.sigmoid(g_ref[...].astype(F32)) * yb).astype(o_ref.dtype)


def _pool_tables(tp, row_len):
    nw = len(POOL_WINDOWS)
    band = np.zeros((nw, tp, tp), np.float32)
    invc = np.zeros((nw, tp, LANES), np.float32)
    j = np.arange(row_len)
    for k, w in enumerate(POOL_WINDOWS):
        lo = np.clip(j - w // 2, 0, row_len - 1)
        hi = np.clip(j + w // 2 - 1, 0, row_len - 1)
        for r0 in range(0, tp, row_len):
            for jj in range(row_len):
                band[k, r0 + jj, r0 + lo[jj]: r0 + hi[jj] + 1] = 1.0
                invc[k, r0 + jj, :] = 1.0 / (hi[jj] - lo[jj] + 1)
    return jnp.asarray(band, dtype=BF16), jnp.asarray(invc)


def _pool(p, merged, w_pool, l, pool_scale, u_off, gate_off, row0, n_rows, row_len):
    _, nw, pg, po = w_pool.shape
    d = merged.shape[1]
    tp = max(row_len, min(512, n_rows))
    band, invc = _pool_tables(tp, row_len)
    ub = u_off // pg
    gb = gate_off // po
    r0 = row0 // tp
    return pl.pallas_call(
        _pool_kernel,
        grid=(n_rows // tp, nw),
        in_specs=[pl.BlockSpec((tp, pg), lambda i, k: (r0 + i, ub + k)),
                  pl.BlockSpec((1, tp, tp), lambda i, k: (k, 0, 0)),
                  pl.BlockSpec((1, tp, LANES), lambda i, k: (k, 0, 0)),
                  pl.BlockSpec((None, 1, pg, po), lambda i, k: (l, k, 0, 0)),
                  pl.BlockSpec((1, po), lambda i, k: (0, k)),
                  pl.BlockSpec((tp, po), lambda i, k: (r0 + i, gb + k)),
                  pl.BlockSpec((tp, po), lambda i, k: (r0 + i, k))],
        out_specs=pl.BlockSpec((tp, po), lambda i, k: (r0 + i, k)),
        out_shape=jax.ShapeDtypeStruct(merged.shape, merged.dtype),
        input_output_aliases={6: 0},
        compiler_params=_params("parallel", "arbitrary"),
        name="pool_branch",
    )(p, band, invc, w_pool, pool_scale.reshape(1, d), p, merged)


def _confconv_kernel(a_ref, b_ref, w_ref, bias_ref, o_ref, scr, *, seq, taps, stride, chunk, pad):
    lanes = a_ref.shape[1]
    first = pad - (taps // 2) * stride
    scr[pl.ds(0, pad), :] = jnp.zeros((pad, lanes), F32)
    scr[pl.ds(pad + seq, pad), :] = jnp.zeros((pad, lanes), F32)
    scr[pl.ds(pad, seq), :] = a_ref[...].astype(F32) * jax.nn.sigmoid(b_ref[...].astype(F32))

    def do_chunk(base):
        acc = jnp.broadcast_to(bias_ref[...], (chunk, lanes))
        for k in range(taps):
            acc = acc + w_ref[k:k + 1, :] * scr[pl.ds(base + first + k * stride, chunk), :]
        o_ref[pl.ds(base, chunk), :] = acc

    n_chunks = seq // chunk
    if n_chunks == 1:
        do_chunk(0)
    else:
        def body(ci, carry):
            do_chunk(pl.multiple_of(ci * chunk, chunk))
            return carry
        lax.fori_loop(0, n_chunks, body, 0)


def _confconv(p, w, b, a_off, row0, seq, n_seq, stride):
    taps, cw = w.shape
    tc = LANES
    chunk = min(seq, 256)
    pad_alloc = -(-((taps // 2) * stride) // 8) * 8
    ab = a_off // tc
    bb = (a_off + cw) // tc
    r0 = row0 // seq
    kern = functools.partial(_confconv_kernel, seq=seq, taps=taps, stride=stride, chunk=chunk,
                             pad=pad_alloc)
    return pl.pallas_call(
        kern,
        grid=(n_seq, cw // tc),
        in_specs=[pl.BlockSpec((seq, tc), lambda s, j: (r0 + s, ab + j)),
                  pl.BlockSpec((seq, tc), lambda s, j: (r0 + s, bb + j)),
                  pl.BlockSpec((taps, tc), lambda s, j: (0, j)),
                  pl.BlockSpec((1, tc), lambda s, j: (0, j))],
        out_specs=pl.BlockSpec((seq, tc), lambda s, j: (s, j)),
        out_shape=jax.ShapeDtypeStruct((n_seq * seq, cw), F32),
        scratch_shapes=[pltpu.VMEM((seq + 2 * pad_alloc, tc), F32)],
        compiler_params=_params("parallel", "parallel"),
        name="conf_conv",
    )(p, p, w, b.reshape(1, cw))


def _ln_silu_kernel(xl_ref, xc_ref, w_ref, b_ref, o_ref, *, n_lat):
    def run(x_ref):
        x = x_ref[...]
        mu = jnp.mean(x, axis=-1, keepdims=True)
        xc = x - mu
        var = jnp.mean(xc * xc, axis=-1, keepdims=True)
        y = xc * lax.rsqrt(var + EPS) * w_ref[...] + b_ref[...]
        o_ref[...] = _silu(y).astype(o_ref.dtype)

    is_lat = pl.program_id(0) < n_lat
    pl.when(is_lat)(lambda: run(xl_ref))
    pl.when(jnp.logical_not(is_lat))(lambda: run(xc_ref))


def _ln_silu(v_lat, v_ctx, w, b):
    cw = v_lat.shape[1]
    tm = ELT_TILE
    n_lat = v_lat.shape[0] // tm
    t = v_lat.shape[0] + v_ctx.shape[0]
    xl_spec, xc_spec = _two_source_specs((tm, cw), n_lat, ())
    return pl.pallas_call(
        functools.partial(_ln_silu_kernel, n_lat=n_lat),
        grid=(t // tm,),
        in_specs=[xl_spec, xc_spec,
                  pl.BlockSpec((1, cw), lambda i: (0, 0)),
                  pl.BlockSpec((1, cw), lambda i: (0, 0))],
        out_specs=pl.BlockSpec((tm, cw), lambda i: (i, 0)),
        out_shape=jax.ShapeDtypeStruct((t, cw), BF16),
        compiler_params=_params("parallel"),
        name="conf_ln_silu",
    )(v_lat, v_ctx, w.reshape(1, cw), b.reshape(1, cw))


def _top1(rows):
    best, idx = rows[0], jnp.zeros(rows[0].shape, jnp.int32)
    for j in range(1, len(rows)):
        better = rows[j] > best
        idx = jnp.where(better, j, idx)
        best = jnp.where(better, rows[j], best)
    return best, idx


def _router_kernel(x_ref, w_ref, sc_ref, sh_ref, wr_ref, rb_ref, h_ref, sel_ref, *, n_experts):
    h2 = _prenorm_math(x_ref[...], w_ref[...], sc_ref[0], sh_ref[0])
    h_hi, h_lo = _split_bf16(h2, 2)
    h_ref[...] = _pack_bf16_pairs(h_hi.astype(F32))
    w_hi, w_lo = _split_bf16(wr_ref[...], 2)
    nt = (((1,), (1,)), ((), ()))

    def dg(a, b):
        return lax.dot_general(a, b, nt, preferred_element_type=F32)

    logits = dg(w_hi, h_hi) + dg(w_hi, h_lo) + dg(w_lo, h_hi)
    scores = jax.nn.sigmoid(logits)
    sel = scores + rb_ref[:, 0:1]
    per_group = n_experts // N_EXPERT_GROUPS
    sel_rows = [sel[e:e + 1, :] for e in range(n_experts)]
    score_rows = [scores[e:e + 1, :] for e in range(n_experts)]
    neg = jnp.full(sel_rows[0].shape, -jnp.inf, F32)

    group_scores = []
    for g in range(N_EXPERT_GROUPS):
        rows = sel_rows[g * per_group:(g + 1) * per_group]
        m1, i1 = _top1(rows)
        m2, _ = _top1([jnp.where(i1 == j, neg, rows[j]) for j in range(per_group)])
        group_scores.append(m1 + m2)
    _, best_group = _top1(group_scores)

    masked = [jnp.where(best_group == e // per_group, sel_rows[e], neg) for e in range(n_experts)]
    _, e1 = _top1(masked)
    _, e2 = _top1([jnp.where(e1 == e, neg, masked[e]) for e in range(n_experts)])
    s1 = sum(jnp.where(e1 == e, score_rows[e], 0.0) for e in range(n_experts))
    s2 = sum(jnp.where(e2 == e, score_rows[e], 0.0) for e in range(n_experts))
    tot = s1 + s2
    sel_ref[...] = jnp.zeros(sel_ref.shape, F32)
    sel_ref[0:1, :] = e1.astype(F32)
    sel_ref[1:2, :] = e2.astype(F32)
    sel_ref[2:3, :] = s1 / tot
    sel_ref[3:4, :] = s2 / tot


def _router(xa, w, mods, sc_col, sh_col, seq, n_batch, w_router, router_bias):
    t, d = xa.shape
    n_experts = w_router.shape[1]
    tm = ELT_TILE
    wr_t = w_router.astype(F32).T
    rb = jnp.broadcast_to(router_bias.astype(F32)[:, None], (n_experts, LANES))
    return pl.pallas_call(
        functools.partial(_router_kernel, n_experts=n_experts),
        grid=(t // tm,),
        in_specs=[pl.BlockSpec((tm, d), lambda i: (i, 0)),
                  pl.BlockSpec((1, d), lambda i: (0, 0)),
                  _mod_spec(d, sc_col, seq, n_batch + 1, tm),
                  _mod_spec(d, sh_col, seq, n_batch + 1, tm),
                  pl.BlockSpec((n_experts, d), lambda i: (0, 0)),
                  pl.BlockSpec((n_experts, LANES), lambda i: (0, 0))],
        out_specs=[pl.BlockSpec((tm, d // 2), lambda i: (i, 0)),
                   pl.BlockSpec((8, tm), lambda i: (0, i))],
        out_shape=[jax.ShapeDtypeStruct((t, d // 2), jnp.uint32),
                   jax.ShapeDtypeStruct((8, t), F32)],
        compiler_params=_params("parallel"),
        name="moe_prenorm_router",
    )(xa, w.reshape(1, d), mods, mods, wr_t, rb)


def _row_copy(src_hbm, row, dst, slot, sem):
    return pltpu.make_async_copy(src_hbm.at[pl.ds(row, 1), :], dst.at[pl.ds(slot, 1), :], sem)


def _gather_rows(src_hbm, idx_ref, dst, sem, n_rows, first=0, priority=0):
    def issue(k, carry):
        r = first + k
        _row_copy(src_hbm, idx_ref[0, 0, r], dst, r, sem).start(priority=priority)
        return carry
    lax.fori_loop(0, n_rows, issue, 0, unroll=GATHER_UNROLL)


def _wait_rows(src_hbm, dst, sem, n_rows):
    pltpu.make_async_copy(src_hbm.at[pl.ds(0, n_rows), :], dst, sem).wait()


def _moe_kernel(te_ref, nv_ref, tok_ref, tok_next_ref, h_hbm, wg_ref, wu_ref, wd_ref, o_ref,
                xbuf, xb, hg, hu, sem, *, tm):
    i = pl.program_id(0)
    f = pl.program_id(1)
    last = pl.num_programs(1) - 1
    n_valid = nv_ref[0]
    valid = i < n_valid
    n_slices, _, kc = xb.shape
    per_half = n_slices // 2
    half = o_ref.shape[1]

    @pl.when(f == 0)
    def _():
        @pl.when(i == 0)
        def _():
            _gather_rows(h_hbm, tok_ref, xbuf, sem, tm)

        @pl.when(valid)
        def _():
            _wait_rows(h_hbm, xbuf, sem, tm)
            lo, hi = _unpack_bf16_pairs(xbuf[...])
            for s in range(n_slices):
                src = lo if s < per_half else hi
                off = (s % per_half) * kc
                xb[s] = src[:, off:off + kc].astype(BF16)

    @pl.when(i + 1 < n_valid)
    def _():
        share = tm // n_slices
        _gather_rows(h_hbm, tok_next_ref, xbuf, sem, share, first=f * share)

    @pl.when(valid)
    def _():
        x = xb[f]
        pg = _dot(x, wg_ref[...])
        pu = _dot(x, wu_ref[...])

        @pl.when(f == 0)
        def _():
            hg[...] = pg
            hu[...] = pu

        @pl.when(jnp.logical_and(f > 0, f < last))
        def _():
            hg[...] += pg
            hu[...] += pu

        @pl.when(f == last)
        def _():
            hid = (_silu(hg[...] + pg) * (hu[...] + pu)).astype(BF16)
            lo = _dot(hid, wd_ref[:, :half]).astype(BF16).astype(F32)
            hi = _dot(hid, wd_ref[:, half:]).astype(BF16).astype(F32)
            o_ref[...] = _pack_halves(lo, hi)

    @pl.when(jnp.logical_and(f == last, jnp.logical_not(valid)))
    def _():
        o_ref[...] = jnp.zeros(o_ref.shape, o_ref.dtype)


def _moe(h2p, tile_expert, n_valid, row_token, wg, wu, wd, l):
    half = h2p.shape[1]
    d = 2 * half
    ff = wg.shape[3]
    kc = d // MOE_K_SPLIT
    assert MOE_K_SPLIT % 2 == 0 and MOE_K_SPLIT >= 2
    tm = MOE_TILE
    n_tiles = tile_expert.shape[0]
    tok = row_token.reshape(n_tiles, 1, tm)
    w_in_spec = pl.BlockSpec((None, None, kc, ff), lambda i, f, te, nv: (l, te[i], f, 0))
    grid_spec = pltpu.PrefetchScalarGridSpec(
        num_scalar_prefetch=2,
        grid=(n_tiles, MOE_K_SPLIT),
        in_specs=[pl.BlockSpec((1, 1, tm), lambda i, f, te, nv: (i, 0, 0), memory_space=pltpu.SMEM),
                  pl.BlockSpec((1, 1, tm), lambda i, f, te, nv: (jnp.minimum(i + 1, n_tiles - 1), 0, 0),
                               memory_space=pltpu.SMEM),
                  pl.BlockSpec(memory_space=pl.ANY),
                  w_in_spec, w_in_spec,
                  pl.BlockSpec((None, None, ff, d), lambda i, f, te, nv: (l, te[i], 0, 0),
                               pipeline_mode=pl.Buffered(1))],
        out_specs=pl.BlockSpec((tm, half), lambda i, f, te, nv: (i, 0)),
        scratch_shapes=[pltpu.VMEM((tm, half), jnp.uint32), pltpu.VMEM((MOE_K_SPLIT, tm, kc), BF16),
                        pltpu.VMEM((tm, ff), F32), pltpu.VMEM((tm, ff), F32),
                        pltpu.SemaphoreType.DMA(())],
    )
    return pl.pallas_call(
        functools.partial(_moe_kernel, tm=tm),
        grid_spec=grid_spec,
        out_shape=jax.ShapeDtypeStruct((n_tiles * tm, half), jnp.uint32),
        compiler_params=_params("arbitrary", "arbitrary"),
        name="moe_experts",
    )(tile_expert, n_valid, tok, tok, h2p, wg, wu, wd)


def _combine_kernel(p0_ref, p1_ref, p0n_ref, p1n_ref, x_ref, gm_ref, gw_ref, nw_ref, sc_ref, sh_ref,
                    ys_hbm, *rest, tc, final):
    outs, (buf, sem) = rest[:-2], rest[-2:]
    i = pl.program_id(0)
    slot = lax.rem(i, 2)
    half = buf.shape[3]

    def gather(pa_ref, pb_ref, s):
        _gather_rows(ys_hbm, pa_ref, buf.at[s, 0], sem.at[s], tc, priority=0)
        _gather_rows(ys_hbm, pb_ref, buf.at[s, 1], sem.at[s], tc, priority=1)

    @pl.when(i == 0)
    def _():
        gather(p0_ref, p1_ref, 0)

    @pl.when(i + 1 < pl.num_programs(0))
    def _():
        gather(p0n_ref, p1n_ref, 1 - slot)

    _wait_rows(ys_hbm, buf.at[slot, 0], sem.at[slot], tc)
    _wait_rows(ys_hbm, buf.at[slot, 1], sem.at[slot], tc)
    lo0, hi0 = _unpack_bf16_pairs(buf[slot, 0])
    lo1, hi1 = _unpack_bf16_pairs(buf[slot, 1])
    w0 = gw_ref[:, 0:1]
    w1 = gw_ref[:, LANES:LANES + 1]
    g = gm_ref[0]
    new_lo = x_ref[:, :half] + g[:, :half] * (w0 * lo0 + w1 * lo1)
    new_hi = x_ref[:, half:] + g[:, half:] * (w0 * hi0 + w1 * hi1)
    ms = (jnp.sum(new_lo * new_lo, axis=-1, keepdims=True)
          + jnp.sum(new_hi * new_hi, axis=-1, keepdims=True)) * (1.0 / (2 * half))
    inv = lax.rsqrt(ms + EPS)
    nw = nw_ref[...]
    if final:
        o_ref, = outs
        o_ref[:, :half] = new_lo * inv * nw[:, :half]
        o_ref[:, half:] = new_hi * inv * nw[:, half:]
    else:
        x_out_ref, h_ref = outs
        x_out_ref[:, :half] = new_lo
        x_out_ref[:, half:] = new_hi
        sc = 1.0 + sc_ref[0]
        sh = sh_ref[0]
        h_ref[:, :half] = (new_lo * inv * nw[:, :half] * sc[:, :half] + sh[:, :half]).astype(h_ref.dtype)
        h_ref[:, half:] = (new_hi * inv * nw[:, half:] * sc[:, half:] + sh[:, half:]).astype(h_ref.dtype)


def _combine(xa, ys, pos0, pos1, gate_cols, mods, mod_col, seq, n_batch, norm_w, next_mods, final,
             n_rows):
    d = xa.shape[1]
    half = d // 2
    tc = ELT_TILE
    n = n_rows // tc
    p0 = pos0[:n_rows].reshape(n, 1, tc)
    p1 = pos1[:n_rows].reshape(n, 1, tc)
    cur = functools.partial(pl.BlockSpec, (1, 1, tc), lambda i: (i, 0, 0), memory_space=pltpu.SMEM)
    nxt = functools.partial(pl.BlockSpec, (1, 1, tc), lambda i: (jnp.minimum(i + 1, n - 1), 0, 0),
                            memory_space=pltpu.SMEM)
    row_spec = pl.BlockSpec((tc, d), lambda i: (i, 0))
    if final:
        out_specs = [row_spec]
        out_shape = [jax.ShapeDtypeStruct((n_rows, d), F32)]
        aliases = {}
    else:
        out_specs = [row_spec, row_spec]
        out_shape = [jax.ShapeDtypeStruct((n_rows, d), F32), jax.ShapeDtypeStruct((n_rows, d), BF16)]
        aliases = {4: 0}
    return pl.pallas_call(
        functools.partial(_combine_kernel, tc=tc, final=final),
        grid=(n,),
        in_specs=[cur(), cur(), nxt(), nxt(),
                  row_spec,
                  _mod_spec(d, mod_col, seq, n_batch + 1, tc),
                  pl.BlockSpec((tc, 2 * LANES), lambda i: (i, 0)),
                  pl.BlockSpec((1, d), lambda i: (0, 0)),
                  _mod_spec(d, 1, seq, n_batch + 1, tc),
                  _mod_spec(d, 0, seq, n_batch + 1, tc),
                  pl.BlockSpec(memory_space=pl.ANY)],
        out_specs=out_specs,
        out_shape=out_shape,
        scratch_shapes=[pltpu.VMEM((2, 2, tc, half), jnp.uint32), pltpu.SemaphoreType.DMA((2,))],
        input_output_aliases=aliases,
        compiler_params=_params("arbitrary"),
        name="moe_combine",
    )(p0, p1, p0, p1, xa, mods, gate_cols, norm_w.reshape(1, d), next_mods, next_mods, ys)


def _routing_tables(sel, n_experts, tm):
    t = sel.shape[1]
    e = sel[0:2].astype(jnp.int32).reshape(-1)
    onehot = (e[:, None] == jnp.arange(n_experts, dtype=jnp.int32)[None, :]).astype(jnp.int32)
    incl = jnp.cumsum(onehot, axis=0)
    rank = jnp.sum((incl - onehot) * onehot, axis=1)
    counts = incl[-1]
    tiles = (counts + tm - 1) // tm
    tile_end = jnp.cumsum(tiles)
    tile_start = tile_end - tiles
    pos = tile_start[e] * tm + rank
    n_tiles = (2 * t) // tm + n_experts
    token = jnp.tile(jnp.arange(t, dtype=jnp.int32), 2)
    row_token = jnp.zeros((n_tiles * tm,), jnp.int32).at[pos].set(token)
    n_valid = tile_end[-1]
    tile_ids = jnp.minimum(jnp.arange(n_tiles, dtype=jnp.int32), n_valid - 1)
    tile_expert = jnp.sum((tile_ids[:, None] >= tile_end[None, :]).astype(jnp.int32), axis=1)
    gate_cols = jnp.broadcast_to(sel[2:4].T[:, :, None], (t, 2, LANES)).reshape(t, 2 * LANES)
    return tile_expert, n_valid.reshape(1), row_token, pos[:t], pos[t:], gate_cols


def kernel(x, c, ctx, c_ctx, w_ada, b_ada, norm1_w, norm2_w, w_in, ssd_conv_w, ssd_conv_b,
           ssd_A_log, ssd_dt_bias, ssd_D, ssd_norm_w, w_ssd_out, w_pool, pool_scale,
           conf_conv_w, conf_conv_b, conf_ln_w, conf_ln_b, w_conf_out, w_out, w_router,
           router_bias, w_exp_gate, w_exp_up, w_exp_down, final_norm_w):
    n_batch, seq, d = x.shape
    ctx_len = ctx.shape[1]
    depth = w_in.shape[0]
    inner = w_ssd_out.shape[1]
    heads = ssd_A_log.shape[-1]
    xbc_w = ssd_conv_w.shape[-1]
    gn = (xbc_w - inner) // 2
    n_groups = gn // SSD_STATE
    hpg = heads // n_groups
    geo = (inner, n_groups, SSD_STATE, hpg, SSD_HEAD_DIM)
    pool_w = w_pool.shape[1] * w_pool.shape[2]
    conf_w = conf_conv_w.shape[-1]
    n_experts = w_router.shape[1]
    assert heads <= LANES and hpg * SSD_HEAD_DIM * n_groups == inner
    assert seq % GRID_W == 0 and seq % ROW_TILE == 0 and ctx_len % ELT_TILE == 0

    off_dt = xbc_w
    off_z = off_dt + 2 * heads
    off_pool = off_z + inner
    off_conf = off_pool + pool_w
    off_gate = off_conf + 2 * conf_w
    p_z = xbc_w
    p_pool = p_z + inner
    p_conf = p_pool + pool_w
    p_gate = p_conf + 2 * conf_w

    t_lat = n_batch * seq
    t_ctx = n_batch * ctx_len
    t_all = t_lat + t_ctx
    assert t_all % ROW_TILE == 0 and t_ctx % ROW_TILE == 0 and t_ctx <= seq
    x_rows = x.reshape(t_lat, d)
    ctx_rows = ctx.reshape(t_ctx, d)

    c_rows = jnp.zeros((BF16_SUBLANES, d), F32).at[:n_batch].set(c).at[n_batch].set(c_ctx)
    h0 = jnp.zeros((n_batch, 2, n_groups, SSD_STATE, hpg * SSD_HEAD_DIM), F32)

    wt_in = jnp.swapaxes(w_in, 1, 2).astype(BF16)
    w_dt = jnp.zeros((depth, d, 2 * LANES), F32)
    w_dt = w_dt.at[:, :, :heads].set(w_in[:, :, off_dt:off_dt + heads])
    w_dt = w_dt.at[:, :, LANES:LANES + heads].set(w_in[:, :, off_dt + heads:off_z]).astype(BF16)
    w_ssd_out_b = w_ssd_out.astype(BF16)
    w_pool_b = w_pool.astype(BF16)
    w_conf_out_b = w_conf_out.astype(BF16)
    w_out_b = w_out.astype(BF16)
    wg_b = w_exp_gate.astype(BF16)
    wu_b = w_exp_up.astype(BF16)
    wd_b = w_exp_down.astype(BF16)

    all_mods = [_ada(c_rows, w_ada, b_ada, l).reshape(BF16_SUBLANES, 1, 6 * d) for l in range(depth)]
    xa = None
    h = _prenorm(x_rows, ctx_rows, norm1_w[0], all_mods[0], 1, 0, seq, n_batch)
    for l in range(depth):
        mods = all_mods[l]

        p = _in_proj(h, wt_in, l, off_dt, off_z - off_dt, p_gate + N_BRANCHES * d)
        dtr = _mm(h, w_dt, l, F32, name="dt_proj")

        xbc_ctx = _seqconv(p, ssd_conv_w[l], ssd_conv_b[l], t_lat, ctx_len, n_batch)
        xbc_lat = _seqconv(p, ssd_conv_w[l], ssd_conv_b[l], 0, seq, n_batch)
        y_ctx, s_ctx = _ssd(xbc_ctx, dtr, t_lat, ssd_A_log[l], ssd_dt_bias[l], ssd_D[l], h0,
                            ctx_len, n_batch, geo)
        y_lat, _ = _ssd(xbc_lat, dtr, 0, ssd_A_log[l], ssd_dt_bias[l], ssd_D[l], s_ctx,
                        seq, n_batch, geo)
        ya = _gate_norm(y_lat, y_ctx, p, p_z, ssd_norm_w[l])
        merged = _mm(ya, w_ssd_out_b, l, BF16, gate=p, gate_off=p_gate, name="ssd_out")

        merged = _pool(p, merged, w_pool_b, l, pool_scale[l], p_pool, p_gate + d, 0, t_lat, GRID_W)
        merged = _pool(p, merged, w_pool_b, l, pool_scale[l], p_pool, p_gate + d, t_lat, t_ctx,
                       ctx_len)

        v_lat = _confconv(p, conf_conv_w[l], conf_conv_b[l], p_conf, 0, seq, n_batch, GRID_W)
        v_ctx = _confconv(p, conf_conv_w[l], conf_conv_b[l], p_conf, t_lat, ctx_len, n_batch, 1)
        ca = _ln_silu(v_lat, v_ctx, conf_ln_w[l], conf_ln_b[l])
        merged = _mm(ca, w_conf_out_b, l, BF16, gate=p, gate_off=p_gate + 2 * d, prev=merged,
                     name="conf_out")

        resid = (x_rows, ctx_rows) if xa is None else xa
        xa = _mm(merged, w_out_b, l, F32, resid=resid, mods=mods, mod_col=2, seq=seq,
                 n_batch=n_batch, name="mixer_out")

        h2p, sel = _router(xa, norm2_w[l], mods, 4, 3, seq, n_batch, w_router, router_bias)
        tile_expert, n_valid, row_token, pos0, pos1, gate_cols = _routing_tables(
            sel, n_experts, MOE_TILE)
        ys = _moe(h2p, tile_expert, n_valid, row_token, wg_b, wu_b, wd_b, l)
        if l + 1 < depth:
            xa, h = _combine(xa, ys, pos0, pos1, gate_cols, mods, 5, seq, n_batch, norm1_w[l + 1],
                             all_mods[l + 1], False, t_all)
        else:
            out, = _combine(xa, ys, pos0, pos1, gate_cols, mods, 5, seq, n_batch, final_norm_w,
                            mods, True, t_lat)
    return out.reshape(n_batch, seq, d)
```
